```python
import math
import jax, jax.numpy as jnp
from jax import lax
import numpy as np

D_MODEL = 1024
BATCH = 16
SEQ = 4096
DEPTH = 4
DEC_BATCH = 32
DEC_SEQ = 64
PAST_LEN = 2048

CHUNK = 64
Q_BLOCK = 128
N_MIXERS = 2
N_DIFF_LAYERS = (DEPTH + 1) // 2
N_FOX_LAYERS = DEPTH // 2
DIFF_HEADS = 8
DIFF_QK_DIM = 64
DIFF_V_DIM = 2 * DIFF_QK_DIM
DIFF_QK_WIDTH = 2 * DIFF_HEADS * DIFF_QK_DIM
DIFF_V_WIDTH = DIFF_HEADS * DIFF_V_DIM
FOX_HEADS = 16
FOX_HEAD_DIM = 64
FOX_WIDTH = FOX_HEADS * FOX_HEAD_DIM
D_FF = ((-(-8 * D_MODEL // 3) + 255) // 256) * 256
ROPE_THETA = 10000.0
NORM_EPS = 1e-6
SUBLN_EPS = 1e-5
LAMBDA_STD = 0.1
FORGET_BIAS_MEAN = 2.0

kernel_name = "diff_fox_streaming_encoder_step"


def _rmsnorm(x, w, eps=NORM_EPS):
    xf = x.astype(jnp.float32)
    y = xf * lax.rsqrt(jnp.mean(xf * xf, axis=-1, keepdims=True) + eps)
    return (y * w.astype(jnp.float32)).astype(x.dtype)


def _rope(x, pos):
    half = x.shape[-1] // 2
    inv_freq = ROPE_THETA ** (-jnp.arange(half, dtype=jnp.float32) / half)
    ang = pos.astype(jnp.float32)[:, None] * inv_freq[None, :]
    cos = jnp.cos(ang)[None, :, None, :]
    sin = jnp.sin(ang)[None, :, None, :]
    xf = x.astype(jnp.float32)
    x1, x2 = xf[..., :half], xf[..., half:]
    return jnp.concatenate([x1 * cos - x2 * sin, x2 * cos + x1 * sin], axis=-1).astype(x.dtype)


def _lambda_init(layer):
    return 0.8 - 0.6 * math.exp(-0.3 * layer)


def _swiglu(h, w_in, w_out):
    gate, up = jnp.split(h @ w_in, 2, axis=-1)
    return (jax.nn.silu(gate) * up) @ w_out


def _diff_attend(q, k, v, q_pos, k_pos, lam):
    s = jnp.einsum('bqhd,bkhd->bhqk', q, k, preferred_element_type=jnp.float32) * (DIFF_QK_DIM ** -0.5)
    allowed = (k_pos[None, :] // CHUNK) <= (q_pos[:, None] // CHUNK)
    p = jax.nn.softmax(jnp.where(allowed[None, None], s, -jnp.inf), axis=-1)
    b, _, tq, tk = p.shape
    p = p.reshape(b, DIFF_HEADS, 2, tq, tk)
    a = p[:, :, 0] - lam * p[:, :, 1]
    return jnp.einsum('bhqk,bkhe->bqhe', a.astype(v.dtype), v)


def _fox_attend(q, k, v, cq, ck, q_pos, k_pos):
    s = jnp.einsum('bqhd,bkhd->bhqk', q, k, preferred_element_type=jnp.float32) * (FOX_HEAD_DIM ** -0.5)
    s = s + jnp.transpose(cq, (0, 2, 1))[:, :, :, None] - jnp.transpose(ck, (0, 2, 1))[:, :, None, :]
    allowed = k_pos[None, :] <= q_pos[:, None]
    p = jax.nn.softmax(jnp.where(allowed[None, None], s, -jnp.inf), axis=-1)
    return jnp.einsum('bhqk,bkhd->bqhd', p.astype(v.dtype), v)


def _diff_mixer(h, pos, w_qkv, w_o, lam_params, subln_w, layer, cache_k=None, cache_v=None):
    b, t, _ = h.shape
    qkv = h @ w_qkv
    q = _rope(qkv[..., :DIFF_QK_WIDTH].reshape(b, t, 2 * DIFF_HEADS, DIFF_QK_DIM), pos)
    k = _rope(qkv[..., DIFF_QK_WIDTH:2 * DIFF_QK_WIDTH].reshape(b, t, 2 * DIFF_HEADS, DIFF_QK_DIM), pos)
    v = qkv[..., 2 * DIFF_QK_WIDTH:].reshape(b, t, DIFF_HEADS, DIFF_V_DIM)
    lam_init = _lambda_init(layer)
    lp = lam_params.astype(jnp.float32)
    lam = jnp.exp(jnp.sum(lp[0] * lp[1])) - jnp.exp(jnp.sum(lp[2] * lp[3])) + lam_init
    if cache_k is None:
        outs = [_diff_attend(q[:, lo:lo + Q_BLOCK], k[:, :lo + Q_BLOCK], v[:, :lo + Q_BLOCK],
                             pos[lo:lo + Q_BLOCK], pos[:lo + Q_BLOCK], lam)
                for lo in range(0, t, Q_BLOCK)]
        o = jnp.concatenate(outs, axis=1)
    else:
        p_len = cache_k.shape[1]
        k_all = jnp.concatenate([cache_k.astype(k.dtype), k], axis=1)
        v_all = jnp.concatenate([cache_v.astype(v.dtype), v], axis=1)
        k_pos = jnp.arange(p_len + t, dtype=jnp.int32)
        o = _diff_attend(q, k_all, v_all, pos, k_pos, lam)
    o = _rmsnorm(o, subln_w, SUBLN_EPS) * (1.0 - lam_init)
    return o.reshape(b, t, DIFF_V_WIDTH) @ w_o, k, v


def _fox_mixer(h, pos, w_qkvf, b_f, w_o, cache_k=None, cache_v=None, cache_logf=None):
    b, t, _ = h.shape
    proj = h @ w_qkvf
    q = proj[..., :FOX_WIDTH].reshape(b, t, FOX_HEADS, FOX_HEAD_DIM)
    k = proj[..., FOX_WIDTH:2 * FOX_WIDTH].reshape(b, t, FOX_HEADS, FOX_HEAD_DIM)
    v = proj[..., 2 * FOX_WIDTH:3 * FOX_WIDTH].reshape(b, t, FOX_HEADS, FOX_HEAD_DIM)
    logf = jax.nn.log_sigmoid(proj[..., 3 * FOX_WIDTH:].astype(jnp.float32) + b_f.astype(jnp.float32))
    if cache_k is None:
        c = jnp.cumsum(logf, axis=1)
        outs = [_fox_attend(q[:, lo:lo + Q_BLOCK], k[:, :lo + Q_BLOCK], v[:, :lo + Q_BLOCK],
                            c[:, lo:lo + Q_BLOCK], c[:, :lo + Q_BLOCK],
                            pos[lo:lo + Q_BLOCK], pos[:lo + Q_BLOCK])
                for lo in range(0, t, Q_BLOCK)]
        o = jnp.concatenate(outs, axis=1)
    else:
        p_len = cache_k.shape[1]
        c = jnp.cumsum(jnp.concatenate([cache_logf.astype(jnp.float32), logf], axis=1), axis=1)
        k_all = jnp.concatenate([cache_k.astype(k.dtype), k], axis=1)
        v_all = jnp.concatenate([cache_v.astype(v.dtype), v], axis=1)
        k_pos = jnp.arange(p_len + t, dtype=jnp.int32)
        o = _fox_attend(q, k_all, v_all, c[:, p_len:], c, pos, k_pos)
    return o.reshape(b, t, FOX_WIDTH) @ w_o, k, v, logf


def setup_inputs(seed: int = 0) -> dict:
    key = jax.random.key(seed)
    ks = jax.random.split(key, 20)
    nrm = jax.random.normal
    f32 = jnp.float32
    return {
        "x_prompt": nrm(ks[0], (BATCH, SEQ, D_MODEL), f32),
        "x_sample": nrm(ks[1], (DEC_BATCH, DEC_SEQ, D_MODEL), f32),
        "cache_diff_k": nrm(ks[2], (N_DIFF_LAYERS, DEC_BATCH, PAST_LEN, 2 * DIFF_HEADS, DIFF_QK_DIM), f32),
        "cache_diff_v": nrm(ks[3], (N_DIFF_LAYERS, DEC_BATCH, PAST_LEN, DIFF_HEADS, DIFF_V_DIM), f32),
        "cache_fox_k": nrm(ks[4], (N_FOX_LAYERS, DEC_BATCH, PAST_LEN, FOX_HEADS, FOX_HEAD_DIM), f32),
        "cache_fox_v": nrm(ks[5], (N_FOX_LAYERS, DEC_BATCH, PAST_LEN, FOX_HEADS, FOX_HEAD_DIM), f32),
        "cache_fox_logf": jax.nn.log_sigmoid(FORGET_BIAS_MEAN + nrm(ks[6], (N_FOX_LAYERS, DEC_BATCH, PAST_LEN, FOX_HEADS), f32)),
        "norm_mix": 1.0 + 0.02 * nrm(ks[7], (DEPTH, D_MODEL), f32),
        "norm_ffn": 1.0 + 0.02 * nrm(ks[8], (DEPTH, D_MODEL), f32),
        "norm_final": 1.0 + 0.02 * nrm(ks[9], (D_MODEL,), f32),
        "diff_w_qkv": nrm(ks[10], (N_DIFF_LAYERS, D_MODEL, 2 * DIFF_QK_WIDTH + DIFF_V_WIDTH), f32) * D_MODEL ** -0.5,
        "diff_w_o": nrm(ks[11], (N_DIFF_LAYERS, DIFF_V_WIDTH, D_MODEL), f32) * DIFF_V_WIDTH ** -0.5,
        "diff_lambda": LAMBDA_STD * nrm(ks[12], (N_DIFF_LAYERS, 4, DIFF_QK_DIM), f32),
        "diff_subln": 1.0 + 0.02 * nrm(ks[13], (N_DIFF_LAYERS, DIFF_V_DIM), f32),
        "fox_w_qkvf": nrm(ks[14], (N_FOX_LAYERS, D_MODEL, 3 * FOX_WIDTH + FOX_HEADS), f32) * D_MODEL ** -0.5,
        "fox_b_f": FORGET_BIAS_MEAN + 0.5 * nrm(ks[15], (N_FOX_LAYERS, FOX_HEADS), f32),
        "fox_w_o": nrm(ks[16], (N_FOX_LAYERS, FOX_WIDTH, D_MODEL), f32) * FOX_WIDTH ** -0.5,
        "ffn_w_in": nrm(ks[17], (DEPTH, D_MODEL, 2 * D_FF), f32) * D_MODEL ** -0.5,
        "ffn_w_out": nrm(ks[18], (DEPTH, D_FF, D_MODEL), f32) * D_FF ** -0.5,
    }


def reference(x_prompt, x_sample, cache_diff_k, cache_diff_v, cache_fox_k, cache_fox_v, cache_fox_logf,
              norm_mix, norm_ffn, norm_final, diff_w_qkv, diff_w_o, diff_lambda, diff_subln,
              fox_w_qkvf, fox_b_f, fox_w_o, ffn_w_in, ffn_w_out):
    s_len = x_prompt.shape[1]
    t_len = x_sample.shape[1]
    p_len = cache_diff_k.shape[2]
    pos_p = jnp.arange(s_len, dtype=jnp.int32)
    pos_s = p_len + jnp.arange(t_len, dtype=jnp.int32)
    xp, xs = x_prompt, x_sample
    dk_p, dv_p, fk_p, fv_p, fl_p = [], [], [], [], []
    dk_s, dv_s, fk_s, fv_s, fl_s = [], [], [], [], []
    for i in range(DEPTH):
        j = i // N_MIXERS
        hp = _rmsnorm(xp, norm_mix[i])
        hs = _rmsnorm(xs, norm_mix[i])
        if i % N_MIXERS == 0:
            yp, kp, vp = _diff_mixer(hp, pos_p, diff_w_qkv[j], diff_w_o[j], diff_lambda[j], diff_subln[j], i)
            ys, ks_, vs_ = _diff_mixer(hs, pos_s, diff_w_qkv[j], diff_w_o[j], diff_lambda[j], diff_subln[j], i,
                                       cache_diff_k[j], cache_diff_v[j])
            dk_p.append(kp); dv_p.append(vp); dk_s.append(ks_); dv_s.append(vs_)
        else:
            yp, kp, vp, lp = _fox_mixer(hp, pos_p, fox_w_qkvf[j], fox_b_f[j], fox_w_o[j])
            ys, ks_, vs_, ls_ = _fox_mixer(hs, pos_s, fox_w_qkvf[j], fox_b_f[j], fox_w_o[j],
                                           cache_fox_k[j], cache_fox_v[j], cache_fox_logf[j])
            fk_p.append(kp); fv_p.append(vp); fl_p.append(lp)
            fk_s.append(ks_); fv_s.append(vs_); fl_s.append(ls_)
        xp = xp + yp
        xs = xs + ys
        xp = xp + _swiglu(_rmsnorm(xp, norm_ffn[i]), ffn_w_in[i], ffn_w_out[i])
        xs = xs + _swiglu(_rmsnorm(xs, norm_ffn[i]), ffn_w_in[i], ffn_w_out[i])
    y_prompt = _rmsnorm(xp, norm_final)
    y_sample = _rmsnorm(xs, norm_final)
    return (y_prompt, y_sample,
            jnp.stack(dk_p), jnp.stack(dv_p), jnp.stack(fk_p), jnp.stack(fv_p), jnp.stack(fl_p),
            jnp.stack(dk_s), jnp.stack(dv_s), jnp.stack(fk_s), jnp.stack(fv_s), jnp.stack(fl_s))
```

```python
import functools
import math

import jax
import jax.numpy as jnp
from jax import lax
from jax.experimental import pallas as pl
from jax.experimental.pallas import tpu as pltpu

D_MODEL = 1024
CHUNK = 64
HEAD_DIM = 64
N_SCORE_HEADS = 16
ROPE_THETA = 10000.0
NORM_EPS = 1e-6
SUBLN_EPS = 1e-5

LANES = 128
N_GROUPS = D_MODEL // LANES
VMEM_LIMIT_BYTES = 56 * 1024 * 1024

F32 = jnp.float32
BF16 = jnp.bfloat16
NEG_INF = float("-inf")


def _row_block(n_rows):
    return 512 if n_rows % 512 == 0 else n_rows


def _params():
    return pltpu.CompilerParams(vmem_limit_bytes=VMEM_LIMIT_BYTES)


def _const_spec(shape):
    nd = len(shape)
    return pl.BlockSpec(shape, lambda *_: (0,) * nd, pipeline_mode=pl.Buffered(1))


def _rmsnorm_rows(x, w, eps):
    ms = jnp.mean(x * x, axis=-1, keepdims=True)
    return x * lax.rsqrt(ms + eps) * w


def _proj_body(*refs, rotary):
    if rotary:
        (x_ref, nw_ref, w_ref, cos_ref, sin_ref,
         qh_ref, kh_ref, vh_ref, k_ref, v_ref) = refs
    else:
        (x_ref, nw_ref, w_ref, bf_ref,
         qh_ref, kh_ref, vh_ref, k_ref, v_ref, lf_ref) = refs
    x = x_ref[...]
    h = _rmsnorm_rows(x, nw_ref[...], NORM_EPS).astype(BF16)
    qkv = jnp.dot(h, w_ref[...], preferred_element_type=F32)
    if rotary:
        cs = cos_ref[...]
        sn = sin_ref[...]
        lane = lax.broadcasted_iota(jnp.int32, cs.shape, 1)
        first_half = (lane & (HEAD_DIM // 2)) == 0
    for g in range(N_GROUPS):
        sl = slice(LANES * g, LANES * (g + 1))
        qg = qkv[:, sl]
        kg = qkv[:, D_MODEL + LANES * g:D_MODEL + LANES * (g + 1)]
        vg = qkv[:, 2 * D_MODEL + LANES * g:2 * D_MODEL + LANES * (g + 1)]
        if rotary:
            qp = jnp.where(first_half, pltpu.roll(qg, LANES - 32, 1), pltpu.roll(qg, 32, 1))
            kp = jnp.where(first_half, pltpu.roll(kg, LANES - 32, 1), pltpu.roll(kg, 32, 1))
            qg = qg * cs + qp * sn
            kg = kg * cs + kp * sn
        qh_ref[g] = qg.astype(BF16)
        kh_ref[g] = kg.astype(BF16)
        vh_ref[g] = vg.astype(BF16)
        k_ref[:, sl] = kg
        v_ref[:, sl] = vg
    if not rotary:
        z = qkv[:, 3 * D_MODEL:] + bf_ref[...]
        lf = jnp.minimum(z, 0.0) - jnp.log1p(jnp.exp(-jnp.abs(z)))
        lf_ref[...] = lf[:, :N_SCORE_HEADS]


def _project(x2d, norm_w, w_bf16, *, rotary, cos=None, sin=None, b_f=None):
    n = x2d.shape[0]
    tm = _row_block(n)
    n_cols = w_bf16.shape[1]
    if rotary:
        t_blocks = cos.shape[0] // tm
        grid = (t_blocks, n // (tm * t_blocks))
        row = lambda t, b: (b * t_blocks + t, 0)
        row3 = lambda t, b: (0, b * t_blocks + t, 0)
        extra_in = [cos, sin]
        extra_specs = [pl.BlockSpec((tm, LANES), lambda t, b: (t, 0)),
                       pl.BlockSpec((tm, LANES), lambda t, b: (t, 0))]
    else:
        grid = (n // tm,)
        row = lambda i: (i, 0)
        row3 = lambda i: (0, i, 0)
        extra_in = [b_f]
        extra_specs = [_const_spec((1, LANES))]
    hm = jax.ShapeDtypeStruct((N_GROUPS, n, LANES), BF16)
    flat = jax.ShapeDtypeStruct((n, D_MODEL), F32)
    hm_spec = pl.BlockSpec((N_GROUPS, tm, LANES), row3)
    flat_spec = pl.BlockSpec((tm, D_MODEL), row)
    out_shape = [hm, hm, hm, flat, flat]
    out_specs = [hm_spec, hm_spec, hm_spec, flat_spec, flat_spec]
    if not rotary:
        out_shape.append(jax.ShapeDtypeStruct((n, N_SCORE_HEADS), F32))
        out_specs.append(pl.BlockSpec((tm, N_SCORE_HEADS), row))
    return pl.pallas_call(
        functools.partial(_proj_body, rotary=rotary),
        grid=grid,
        in_specs=[pl.BlockSpec((tm, D_MODEL), row),
                  _const_spec((1, D_MODEL)),
                  _const_spec((D_MODEL, n_cols))] + extra_specs,
        out_specs=out_specs,
        out_shape=out_shape,
        compiler_params=_params(),
    )(x2d, norm_w, w_bf16, *extra_in)


def _cumsum_body(x_ref, o_ref, *, chunk):
    t = x_ref.shape[2]
    r = lax.broadcasted_iota(jnp.int32, (chunk, chunk), 0)
    c = lax.broadcasted_iota(jnp.int32, (chunk, chunk), 1)
    upper = (r <= c).astype(BF16)
    carry = jnp.zeros((x_ref.shape[1], 1), F32)
    for i in range(t // chunk):
        x = x_ref[0, :, chunk * i:chunk * (i + 1)]
        hi = x.astype(BF16)
        r1 = x - hi.astype(F32)
        mid = r1.astype(BF16)
        lo = (r1 - mid.astype(F32)).astype(BF16)
        cs = (jnp.dot(hi, upper, preferred_element_type=F32)
              + jnp.dot(mid, upper, preferred_element_type=F32)
              + jnp.dot(lo, upper, preferred_element_type=F32)) + carry
        o_ref[0, :, chunk * i:chunk * (i + 1)] = cs
        carry = cs[:, chunk - 1:chunk]


def _cumsum_time(lf_t, chunk):
    b, h, t = lf_t.shape
    return pl.pallas_call(
        functools.partial(_cumsum_body, chunk=chunk),
        grid=(b,),
        in_specs=[pl.BlockSpec((1, h, t), lambda i: (i, 0, 0))],
        out_specs=pl.BlockSpec((1, h, t), lambda i: (i, 0, 0)),
        out_shape=jax.ShapeDtypeStruct((b, h, t), F32),
        compiler_params=_params(),
    )(lf_t)


def _diff_lambda(lam_ref, lam_init):
    lp = lam_ref[...]
    a = jnp.sum(lp[0:1] * lp[1:2], axis=-1, keepdims=True)
    b = jnp.sum(lp[2:3] * lp[3:4], axis=-1, keepdims=True)
    return jnp.exp(a) - jnp.exp(b) + lam_init


def _diff_combine(o1, o2, lam, subln_w, lam_init):
    o = o1 - lam * o2
    return _rmsnorm_rows(o, subln_w, SUBLN_EPS) * (1.0 - lam_init)


def _head_masks(shape):
    lane = lax.broadcasted_iota(jnp.int32, shape, 1)
    return lane < HEAD_DIM


def _prompt_attn_body(*refs, diff, lam_init, tq, tk):
    if diff:
        q_ref, k_ref, v_ref, lam_ref, sub_ref, o_ref, s_ref = refs
    else:
        q_ref, k_ref, v_ref, cq_ref, ck_ref, o_ref, s_ref = refs
    t = q_ref.shape[1]
    n_q = t // tq
    r_blk = tq // tk
    row = lax.broadcasted_iota(jnp.int32, (tq, tk), 0)
    col = lax.broadcasted_iota(jnp.int32, (tq, tk), 1)
    low = _head_masks((tq, LANES))
    if diff:
        lam = _diff_lambda(lam_ref, lam_init)

    def q_step(i, _):
        q0 = pl.multiple_of(i * tq, tq)
        qb = q_ref[0, pl.ds(q0, tq), :]
        zero = jnp.zeros_like(qb)
        heads = []
        for hd in range(2):
            qh = jnp.where(low, qb, zero) if hd == 0 else jnp.where(low, zero, qb)
            if not diff:
                cq = cq_ref[0, 0, pl.ds(q0, tq), hd:hd + 1]

            def score(j, masked):
                k0 = pl.multiple_of(j * tk, tk)
                kb = k_ref[0, pl.ds(k0, tk), :]
                s = lax.dot_general(qh, kb, (((1,), (1,)), ((), ())),
                                    preferred_element_type=F32)
                if not diff:
                    s = s + (cq - ck_ref[0, 0, hd, pl.ds(j, 1), :])
                if masked:
                    d = (j - i * r_blk) * tk
                    if diff:
                        ok = ((col + d) // CHUNK) <= (row // CHUNK)
                    else:
                        ok = (col + d) <= row
                    s = jnp.where(ok, s, NEG_INF)
                return s

            def fold_max(m, s):
                for c in range(tk // LANES):
                    m = jnp.maximum(m, s[:, LANES * c:LANES * (c + 1)])
                return m

            def pass1(j, m):
                s = score(j, False)
                s_ref[j] = s
                return fold_max(m, s)

            m = lax.fori_loop(0, i * r_blk, pass1, jnp.full((tq, LANES), NEG_INF, F32))
            for jj in range(r_blk):
                j = i * r_blk + jj
                s = score(j, True)
                s_ref[j] = s
                m = fold_max(m, s)
            mrow = jnp.max(m, axis=-1, keepdims=True)

            def pass2(j, carry):
                acc, l = carry
                k0 = pl.multiple_of(j * tk, tk)
                p = jnp.exp(s_ref[j] - mrow)
                for c in range(tk // LANES):
                    l = l + p[:, LANES * c:LANES * (c + 1)]
                acc = acc + jnp.dot(p.astype(BF16), v_ref[0, pl.ds(k0, tk), :],
                                    preferred_element_type=F32)
                return acc, l

            acc, l = lax.fori_loop(0, (i + 1) * r_blk, pass2,
                                   (jnp.zeros((tq, LANES), F32), jnp.zeros((tq, LANES), F32)))
            heads.append(acc / jnp.sum(l, axis=-1, keepdims=True))
        if diff:
            out = _diff_combine(heads[0], heads[1], lam, sub_ref[...], lam_init)
        else:
            out = jnp.where(low, heads[0], heads[1])
        o_ref[0, pl.ds(q0, tq), :] = out.astype(BF16)
        return 0

    lax.fori_loop(0, n_q, q_step, 0)


def _prompt_attention(qh, kh, vh, batch, *, diff, lam_init=0.0, lam_params=None, subln_w=None,
                      cq=None, ck=None):
    n = qh.shape[1]
    t = n // batch
    tq = min(512, t)
    tk = min(512, t)
    seq = pl.BlockSpec((1, t, LANES), lambda g, b: (g, b, 0))
    if diff:
        extra_in = [lam_params, subln_w]
        extra_specs = [_const_spec(lam_params.shape), _const_spec(subln_w.shape)]
    else:
        extra_in = [cq, ck]
        extra_specs = [pl.BlockSpec((1, 1, t, 2), lambda g, b: (b, g, 0, 0)),
                       pl.BlockSpec((1, 1, 2, t // tk, tk), lambda g, b: (b, g, 0, 0, 0))]
    return pl.pallas_call(
        functools.partial(_prompt_attn_body, diff=diff, lam_init=lam_init, tq=tq, tk=tk),
        grid=(N_GROUPS, batch),
        in_specs=[seq, seq, seq] + extra_specs,
        out_specs=seq,
        out_shape=jax.ShapeDtypeStruct((N_GROUPS, n, LANES), BF16),
        scratch_shapes=[pltpu.VMEM((t // tk, tq, tk), F32)],
        compiler_params=_params(),
    )(qh, kh, vh, *extra_in)


def _sample_attn_body(*refs, diff, lam_init):
    if diff:
        q_ref, kn_ref, vn_ref, kc_ref, vc_ref, lam_ref, sub_ref, o_ref = refs
    else:
        q_ref, kn_ref, vn_ref, kc_ref, vc_ref, cq_ref, cn_ref, cc_ref, o_ref = refs
    ts = q_ref.shape[1]
    p_len = kc_ref.shape[2]
    qb = q_ref[0]
    kc = kc_ref[0, 0].astype(BF16)
    vc = vc_ref[0, 0].astype(BF16)
    kn = kn_ref[0]
    vn = vn_ref[0]
    low = _head_masks((ts, LANES))
    zero = jnp.zeros_like(qb)
    nt = (((1,), (1,)), ((), ()))
    heads = []
    for hd in range(2):
        qh = jnp.where(low, qb, zero) if hd == 0 else jnp.where(low, zero, qb)
        sc = lax.dot_general(qh, kc, nt, preferred_element_type=F32)
        sn = lax.dot_general(qh, kn, nt, preferred_element_type=F32)
        if not diff:
            cq = cq_ref[0, 0, :, hd:hd + 1]
            cc = cc_ref[0, 0, hd:hd + 1, :]
            sc = sc + ((cq + cc[:, p_len - 1:p_len]) - cc)
            sn = sn + (cq - cn_ref[0, 0, hd:hd + 1, :])
            row = lax.broadcasted_iota(jnp.int32, (ts, ts), 0)
            col = lax.broadcasted_iota(jnp.int32, (ts, ts), 1)
            sn = jnp.where(col <= row, sn, NEG_INF)
        m = jnp.maximum(jnp.max(sc, axis=-1, keepdims=True), jnp.max(sn, axis=-1, keepdims=True))
        pc = jnp.exp(sc - m)
        pn = jnp.exp(sn - m)
        l = jnp.sum(pc, axis=-1, keepdims=True) + jnp.sum(pn, axis=-1, keepdims=True)
        acc = (jnp.dot(pc.astype(BF16), vc, preferred_element_type=F32)
               + jnp.dot(pn.astype(BF16), vn, preferred_element_type=F32))
        heads.append(acc / l)
    if diff:
        lam = _diff_lambda(lam_ref, lam_init)
        out = _diff_combine(heads[0], heads[1], lam, sub_ref[...], lam_init)
    else:
        out = jnp.where(low, heads[0], heads[1])
    o_ref[0] = out.astype(BF16)


def _sample_attention(qh, kh, vh, cache_k, cache_v, layer, *, diff, lam_init=0.0, lam_params=None,
                      subln_w=None, cq=None, cn=None, cc=None):
    _, batch, p_len, _ = cache_k.shape
    n = qh.shape[1]
    ts = n // batch
    new = pl.BlockSpec((1, ts, LANES), lambda g, b: (g, b, 0))
    cache = pl.BlockSpec((1, 1, p_len, LANES), lambda g, b: (layer, b, 0, g))
    if diff:
        extra_in = [lam_params, subln_w]
        extra_specs = [_const_spec(lam_params.shape), _const_spec(subln_w.shape)]
    else:
        extra_in = [cq, cn, cc]
        extra_specs = [pl.BlockSpec((1, 1, ts, 2), lambda g, b: (b, g, 0, 0)),
                       pl.BlockSpec((1, 1, 2, ts), lambda g, b: (b, g, 0, 0)),
                       pl.BlockSpec((1, 1, 2, p_len), lambda g, b: (b, g, 0, 0))]
    return pl.pallas_call(
        functools.partial(_sample_attn_body, diff=diff, lam_init=lam_init),
        grid=(N_GROUPS, batch),
        in_specs=[new, new, new, cache, cache] + extra_specs,
        out_specs=new,
        out_shape=jax.ShapeDtypeStruct((N_GROUPS, n, LANES), BF16),
        compiler_params=_params(),
    )(qh, kh, vh, cache_k, cache_v, *extra_in)


def _mlp_body(x_ref, o_ref, wo_ref, nw_ref, win_ref, wout_ref, nf_ref, y_ref, *, final):
    d_ff = wout_ref.shape[0]
    o = jnp.concatenate([o_ref[g] for g in range(N_GROUPS)], axis=1)
    x1 = x_ref[...] + jnp.dot(o, wo_ref[...], preferred_element_type=F32)
    h = _rmsnorm_rows(x1, nw_ref[...], NORM_EPS).astype(BF16)
    gate = jnp.dot(h, win_ref[:, :d_ff], preferred_element_type=F32)
    up = jnp.dot(h, win_ref[:, d_ff:], preferred_element_type=F32)
    act = (gate * jax.nn.sigmoid(gate) * up).astype(BF16)
    x2 = x1 + jnp.dot(act, wout_ref[...], preferred_element_type=F32)
    if final:
        x2 = _rmsnorm_rows(x2, nf_ref[...], NORM_EPS)
    y_ref[...] = x2


def _mlp(x2d, o_hm, w_o, norm_w, w_in, w_out, norm_final, *, final):
    n = x2d.shape[0]
    tm = _row_block(n)
    d_ff = w_out.shape[0]
    return pl.pallas_call(
        functools.partial(_mlp_body, final=final),
        grid=(n // tm,),
        in_specs=[pl.BlockSpec((tm, D_MODEL), lambda i: (i, 0)),
                  pl.BlockSpec((N_GROUPS, tm, LANES), lambda i: (0, i, 0)),
                  _const_spec((D_MODEL, D_MODEL)),
                  _const_spec((1, D_MODEL)),
                  _const_spec((D_MODEL, 2 * d_ff)),
                  _const_spec((d_ff, D_MODEL)),
                  _const_spec((1, D_MODEL))],
        out_specs=pl.BlockSpec((tm, D_MODEL), lambda i: (i, 0)),
        out_shape=jax.ShapeDtypeStruct((n, D_MODEL), F32),
        compiler_params=_params(),
    )(x2d, o_hm, w_o, norm_w, w_in, w_out, norm_final)


def _rope_tables(pos):
    half = HEAD_DIM // 2
    inv_freq = ROPE_THETA ** (-jnp.arange(half, dtype=F32) / half)
    ang = pos.astype(F32)[:, None] * inv_freq[None, :]
    cos = jnp.tile(jnp.cos(ang), (1, LANES // half))
    sin = jnp.sin(ang)
    sin = jnp.tile(jnp.concatenate([-sin, sin], axis=1), (1, LANES // HEAD_DIM))
    return cos, sin


def _lambda_init(layer):
    return 0.8 - 0.6 * math.exp(-0.3 * layer)


def _pair_major(c_t, batch):
    t = c_t.shape[-1]
    rows = c_t.reshape(batch, N_GROUPS, 2, t)
    return rows, jnp.swapaxes(rows, 2, 3)


def kernel(x_prompt, x_sample, cache_diff_k, cache_diff_v, cache_fox_k, cache_fox_v, cache_fox_logf,
           norm_mix, norm_ffn, norm_final, diff_w_qkv, diff_w_o, diff_lambda, diff_subln,
           fox_w_qkvf, fox_b_f, fox_w_o, ffn_w_in, ffn_w_out):
    bp, s_len, _ = x_prompt.shape
    bs, t_len, _ = x_sample.shape
    p_len = cache_diff_k.shape[2]
    depth = norm_mix.shape[0]
    n_diff = diff_w_qkv.shape[0]
    n_fox = fox_w_qkvf.shape[0]

    xp = x_prompt.reshape(bp * s_len, D_MODEL)
    xs = x_sample.reshape(bs * t_len, D_MODEL)
    tm_s = _row_block(bs * t_len)

    cos_p, sin_p = _rope_tables(jnp.arange(s_len, dtype=jnp.int32))
    cos_s, sin_s = _rope_tables(p_len + jnp.arange(t_len, dtype=jnp.int32))
    cos_s = jnp.tile(cos_s, (tm_s // t_len, 1))
    sin_s = jnp.tile(sin_s, (tm_s // t_len, 1))

    scale = jnp.concatenate([jnp.full((D_MODEL,), HEAD_DIM ** -0.5, F32),
                             jnp.ones((2 * D_MODEL,), F32)])
    w_qkv_d = (diff_w_qkv * scale).astype(BF16)
    pad = LANES - N_SCORE_HEADS
    w_qkv_f = jnp.concatenate([fox_w_qkvf[:, :, :3 * D_MODEL] * scale, fox_w_qkvf[:, :, 3 * D_MODEL:],
                               jnp.zeros((n_fox, D_MODEL, pad), F32)], axis=-1).astype(BF16)
    b_f = jnp.concatenate([fox_b_f, jnp.zeros((n_fox, pad), F32)], axis=-1).reshape(n_fox, 1, LANES)
    w_o_d = diff_w_o.astype(BF16)
    w_o_f = fox_w_o.astype(BF16)
    w_in = ffn_w_in.astype(BF16)
    w_out = ffn_w_out.astype(BF16)
    norm_mix = norm_mix.reshape(depth, 1, D_MODEL)
    norm_ffn = norm_ffn.reshape(depth, 1, D_MODEL)
    norm_final = norm_final.reshape(1, D_MODEL)

    ckd = cache_diff_k.reshape(n_diff, bs, p_len, D_MODEL)
    cvd = cache_diff_v.reshape(n_diff, bs, p_len, D_MODEL)
    ckf = cache_fox_k.reshape(n_fox, bs, p_len, D_MODEL)
    cvf = cache_fox_v.reshape(n_fox, bs, p_len, D_MODEL)

    outs_p = {k: [] for k in ("dk", "dv", "fk", "fv", "fl")}
    outs_s = {k: [] for k in ("dk", "dv", "fk", "fv", "fl")}
    for i in range(depth):
        j = i // 2
        final = i == depth - 1
        if i % 2 == 0:
            lam_init = _lambda_init(i)
            sub = diff_subln[j].reshape(1, LANES)
            qh, kh, vh, k, v = _project(xp, norm_mix[i], w_qkv_d[j], rotary=True, cos=cos_p, sin=sin_p)
            op = _prompt_attention(qh, kh, vh, bp, diff=True, lam_init=lam_init,
                                   lam_params=diff_lambda[j], subln_w=sub)
            outs_p["dk"].append(k)
            outs_p["dv"].append(v)
            qh, kh, vh, k, v = _project(xs, norm_mix[i], w_qkv_d[j], rotary=True, cos=cos_s, sin=sin_s)
            os_ = _sample_attention(qh, kh, vh, ckd, cvd, j, diff=True, lam_init=lam_init,
                                    lam_params=diff_lambda[j], subln_w=sub)
            outs_s["dk"].append(k)
            outs_s["dv"].append(v)
            w_o = w_o_d[j]
        else:
            qh, kh, vh, k, v, lf = _project(xp, norm_mix[i], w_qkv_f[j], rotary=False, b_f=b_f[j])
            lf_p = lf.reshape(bp, s_len, N_SCORE_HEADS)
            c_t = _cumsum_time(jnp.swapaxes(lf_p, 1, 2), min(512, s_len))
            ck, cq = _pair_major(c_t, bp)
            tk = min(512, s_len)
            op = _prompt_attention(qh, kh, vh, bp, diff=False, cq=cq,
                                   ck=ck.reshape(bp, N_GROUPS, 2, s_len // tk, tk))
            outs_p["fk"].append(k)
            outs_p["fv"].append(v)
            outs_p["fl"].append(lf_p)
            qh, kh, vh, k, v, lf = _project(xs, norm_mix[i], w_qkv_f[j], rotary=False, b_f=b_f[j])
            lf_s = lf.reshape(bs, t_len, N_SCORE_HEADS)
            cc, _ = _pair_major(_cumsum_time(jnp.swapaxes(cache_fox_logf[j], 1, 2), min(512, p_len)), bs)
            cn, cq = _pair_major(_cumsum_time(jnp.swapaxes(lf_s, 1, 2), t_len), bs)
            os_ = _sample_attention(qh, kh, vh, ckf, cvf, j, diff=False, cq=cq, cn=cn, cc=cc)
            outs_s["fk"].append(k)
            outs_s["fv"].append(v)
            outs_s["fl"].append(lf_s)
            w_o = w_o_f[j]
        xp = _mlp(xp, op, w_o, norm_ffn[i], w_in[i], w_out[i], norm_final, final=final)
        xs = _mlp(xs, os_, w_o, norm_ffn[i], w_in[i], w_out[i], norm_final, final=final)

    def stack(lst, b, t, h, d):
        return jnp.stack(lst).reshape(len(lst), b, t, h, d)

    dh = D_MODEL // (N_SCORE_HEADS // 2)
    return (xp.reshape(bp, s_len, D_MODEL), xs.reshape(bs, t_len, D_MODEL),
            stack(outs_p["dk"], bp, s_len, N_SCORE_HEADS, HEAD_DIM),
            stack(outs_p["dv"], bp, s_len, N_SCORE_HEADS // 2, dh),
            stack(outs_p["fk"], bp, s_len, N_SCORE_HEADS, HEAD_DIM),
            stack(outs_p["fv"], bp, s_len, N_SCORE_HEADS, HEAD_DIM),
            jnp.stack(outs_p["fl"]),
            stack(outs_s["dk"], bs, t_len, N_SCORE_HEADS, HEAD_DIM),
            stack(outs_s["dv"], bs, t_len, N_SCORE_HEADS // 2, dh),
            stack(outs_s["fk"], bs, t_len, N_SCORE_HEADS, HEAD_DIM),
            stack(outs_s["fv"], bs, t_len, N_SCORE_HEADS, HEAD_DIM),
            jnp.stack(outs_s["fl"]))
```

```python
import functools
import math

import jax
import jax.numpy as jnp
from jax import lax
from jax.experimental import pallas as pl
from jax.experimental.pallas import tpu as pltpu

D_MODEL = 1024
CHUNK = 64
HEAD_DIM = 64
N_SCORE_HEADS = 16
ROPE_THETA = 10000.0
NORM_EPS = 1e-6
SUBLN_EPS = 1e-5

LANES = 128
N_GROUPS = D_MODEL // LANES
VMEM_LIMIT_BYTES = 56 * 1024 * 1024
ROW_BLOCK = 512
KEY_BLOCKS_PER_STEP = 2

F32 = jnp.float32
BF16 = jnp.bfloat16
NEG_INF = float("-inf")
LOG2_E = math.log2(math.e)
NT_DIMS = (((1,), (1,)), ((), ()))


def _params():
    return pltpu.CompilerParams(vmem_limit_bytes=VMEM_LIMIT_BYTES)


def _const_spec(shape):
    nd = len(shape)
    return pl.BlockSpec(shape, lambda *_: (0,) * nd, pipeline_mode=pl.Buffered(1))


def _rmsnorm_rows(x, w, eps):
    ms = jnp.mean(x * x, axis=-1, keepdims=True)
    return x * lax.rsqrt(ms + eps) * w


def _proj_body(*refs, rotary, n_prev):
    if rotary:
        (x_ref, nw_ref, wq_ref, wkt_ref, wv_ref, cos_ref, sin_ref, cost_ref, sint_ref) = refs[:9]
        qh_ref, kth_ref, vh_ref, kt_ref, v_ref = refs[9 + n_prev:]
    else:
        (x_ref, nw_ref, wq_ref, wkt_ref, wvt_ref, wft_ref, bf_ref) = refs[:7]
        qh_ref, kth_ref, vth_ref, kt_ref, vt_ref, lft_ref = refs[7 + n_prev:]
    tm = x_ref.shape[1]
    h = _rmsnorm_rows(x_ref[0], nw_ref[...], NORM_EPS).astype(BF16)
    q = jnp.dot(h, wq_ref[...], preferred_element_type=F32)
    kt = lax.dot_general(wkt_ref[...], h, NT_DIMS, preferred_element_type=F32)
    half = HEAD_DIM // 2
    if rotary:
        cs = cos_ref[...]
        sn = sin_ref[...]
        lane = lax.broadcasted_iota(jnp.int32, cs.shape, 1)
        first_half = (lane & half) == 0
        for g in range(N_GROUPS):
            qg = q[:, LANES * g:LANES * (g + 1)]
            qp = jnp.where(first_half, pltpu.roll(qg, LANES - half, 1), pltpu.roll(qg, half, 1))
            qh_ref[g] = (qg * cs + qp * sn).astype(BF16)
        ct = cost_ref[...]
        st = sint_ref[...]
        for hh in range(N_SCORE_HEADS):
            r0 = HEAD_DIM * hh
            a = kt[r0:r0 + half]
            b = kt[r0 + half:r0 + HEAD_DIM]
            ka = a * ct - b * st
            kb = b * ct + a * st
            kt_ref[0, 0, r0:r0 + half, :] = ka
            kt_ref[0, 0, r0 + half:r0 + HEAD_DIM, :] = kb
            kth_ref[0, 0, r0:r0 + half, :] = ka.astype(BF16)
            kth_ref[0, 0, r0 + half:r0 + HEAD_DIM, :] = kb.astype(BF16)
        v = jnp.dot(h, wv_ref[...], preferred_element_type=F32)
        for g in range(N_GROUPS):
            vg = v[:, LANES * g:LANES * (g + 1)]
            vh_ref[g] = vg.astype(BF16)
            v_ref[0, pl.ds(g, tm, stride=N_GROUPS), :] = vg
    else:
        for g in range(N_GROUPS):
            qh_ref[g] = q[:, LANES * g:LANES * (g + 1)].astype(BF16)
        kt_ref[0, 0] = kt
        kth_ref[0, 0] = kt.astype(BF16)
        vt = lax.dot_general(wvt_ref[...], h, NT_DIMS, preferred_element_type=F32)
        vt_ref[0, 0] = vt
        vth_ref[0, 0] = vt.astype(BF16)
        z = lax.dot_general(wft_ref[...], h, NT_DIMS, preferred_element_type=F32) + bf_ref[...]
        lf = jnp.minimum(z, 0.0) - jnp.log1p(jnp.exp(-jnp.abs(z)))
        lft_ref[0, 0] = lf[:N_SCORE_HEADS]


def _project(x3, norm_w, weights, *, rotary, tables=None, b_f=None, layer=0, n_layers=1, prev=None):
    bx, tx, _ = x3.shape
    tm = min(ROW_BLOCK, tx)
    n = bx * tx
    t_blocks = tx // tm
    grid = (t_blocks, bx)
    prev = list(prev) if prev is not None else []
    n_in_fixed = 9 if rotary else 7
    xspec = pl.BlockSpec((1, tm, D_MODEL), lambda t, b: (b, t, 0))
    wspecs = [_const_spec(w.shape) for w in weights]
    hm = jax.ShapeDtypeStruct((N_GROUPS, n, LANES), BF16)
    hm_spec = pl.BlockSpec((N_GROUPS, tm, LANES), lambda t, b: (0, b * t_blocks + t, 0))
    fm16 = jax.ShapeDtypeStruct((bx, t_blocks, D_MODEL, tm), BF16)
    fm16_spec = pl.BlockSpec((1, 1, D_MODEL, tm), lambda t, b: (b, t, 0, 0))
    fm32 = jax.ShapeDtypeStruct((n_layers, bx, D_MODEL, tx), F32)
    fm32_spec = pl.BlockSpec((1, 1, D_MODEL, tm), lambda t, b: (layer, b, 0, t))
    if rotary:
        cos, sin, cos_t, sin_t = tables
        extra_in = [cos, sin, cos_t, sin_t]
        extra_specs = [pl.BlockSpec((tm, LANES), lambda t, b: (t, 0)),
                       pl.BlockSpec((tm, LANES), lambda t, b: (t, 0)),
                       pl.BlockSpec((HEAD_DIM // 2, tm), lambda t, b: (0, t)),
                       pl.BlockSpec((HEAD_DIM // 2, tm), lambda t, b: (0, t))]
        out_shape = [hm, fm16, hm, fm32,
                     jax.ShapeDtypeStruct((n_layers, n * N_GROUPS, LANES), F32)]
        out_specs = [hm_spec, fm16_spec, hm_spec, fm32_spec,
                     pl.BlockSpec((1, tm * N_GROUPS, LANES), lambda t, b: (layer, b * t_blocks + t, 0))]
        n_f32 = 2
    else:
        extra_in = [b_f]
        extra_specs = [_const_spec(b_f.shape)]
        out_shape = [hm, fm16, fm16, fm32, fm32,
                     jax.ShapeDtypeStruct((n_layers, bx, N_SCORE_HEADS, tx), F32)]
        out_specs = [hm_spec, fm16_spec, fm16_spec, fm32_spec, fm32_spec,
                     pl.BlockSpec((1, 1, N_SCORE_HEADS, tm), lambda t, b: (layer, b, 0, t))]
        n_f32 = 3
    assert len(prev) in (0, n_f32)
    aliases = {n_in_fixed + k: 3 + k for k in range(len(prev))}
    return pl.pallas_call(
        functools.partial(_proj_body, rotary=rotary, n_prev=len(prev)),
        grid=grid,
        in_specs=[xspec, _const_spec((1, D_MODEL))] + wspecs + extra_specs
                 + [pl.BlockSpec(memory_space=pl.ANY)] * len(prev),
        out_specs=out_specs,
        out_shape=out_shape,
        input_output_aliases=aliases,
        compiler_params=_params(),
    )(x3, norm_w, *weights, *extra_in, *prev)


def _cumsum_body(x_ref, o_ref, *, chunk):
    t = x_ref.shape[2]
    r = lax.broadcasted_iota(jnp.int32, (chunk, chunk), 0)
    c = lax.broadcasted_iota(jnp.int32, (chunk, chunk), 1)
    upper = (r <= c).astype(BF16)
    carry = jnp.zeros((x_ref.shape[1], 1), F32)
    for i in range(t // chunk):
        x = x_ref[0, :, chunk * i:chunk * (i + 1)]
        hi = x.astype(BF16)
        r1 = x - hi.astype(F32)
        mid = r1.astype(BF16)
        lo = (r1 - mid.astype(F32)).astype(BF16)
        cs = (jnp.dot(hi, upper, preferred_element_type=F32)
              + jnp.dot(mid, upper, preferred_element_type=F32)
              + jnp.dot(lo, upper, preferred_element_type=F32)) + carry
        o_ref[0, :, chunk * i:chunk * (i + 1)] = cs * LOG2_E
        carry = cs[:, chunk - 1:chunk]


def _cumsum_time(lf_t, chunk):
    b, h, t = lf_t.shape
    return pl.pallas_call(
        functools.partial(_cumsum_body, chunk=chunk),
        grid=(b,),
        in_specs=[pl.BlockSpec((1, h, t), lambda i: (i, 0, 0))],
        out_specs=pl.BlockSpec((1, h, t), lambda i: (i, 0, 0)),
        out_shape=jax.ShapeDtypeStruct((b, h, t), F32),
        compiler_params=_params(),
    )(lf_t)


def _diff_lambda(lam_ref, lam_init):
    lp = lam_ref[...]
    a = jnp.sum(lp[0:1] * lp[1:2], axis=-1, keepdims=True)
    b = jnp.sum(lp[2:3] * lp[3:4], axis=-1, keepdims=True)
    return jnp.exp(a) - jnp.exp(b) + lam_init


def _diff_combine(o1, o2, lam, subln_w, lam_init):
    o = o1 - lam * o2
    return _rmsnorm_rows(o, subln_w, SUBLN_EPS) * (1.0 - lam_init)


def _stack_heads(qb):
    low = lax.broadcasted_iota(jnp.int32, qb.shape, 1) < HEAD_DIM
    zero = jnp.zeros_like(qb)
    return jnp.concatenate([jnp.where(low, qb, zero), jnp.where(low, zero, qb)], axis=0)


def _fold_lanes(x, op):
    acc = x[:, :LANES]
    for c in range(1, x.shape[1] // LANES):
        acc = op(acc, x[:, LANES * c:LANES * (c + 1)])
    return acc


def _merge_heads(o2, t, diff, lam, subln_w, lam_init):
    o_a, o_b = o2[:t], o2[t:]
    if diff:
        out = _diff_combine(o_a, o_b, lam, subln_w, lam_init)
    else:
        low = lax.broadcasted_iota(jnp.int32, o_a.shape, 1) < HEAD_DIM
        out = jnp.where(low, o_a, o_b)
    return out.astype(BF16)


def _prompt_attn_body(*refs, diff, lam_init, blk):
    if diff:
        q_ref, kt_ref, v_ref, lam_ref, sub_ref, o_ref, s_ref, va_ref = refs
        va_ref[:, :LANES] = v_ref[0]
        va_ref[:, LANES:] = jnp.ones((va_ref.shape[0], LANES), BF16)
    else:
        q_ref, kt_ref, vt_ref, cq_ref, ck_ref, o_ref, s_ref, va_ref = refs
        va_ref[:, :LANES, :] = vt_ref[0]
        va_ref[:, LANES:, :] = jnp.ones((va_ref.shape[0], LANES, va_ref.shape[2]), BF16)
    t = q_ref.shape[1]
    g = pl.program_id(0)
    bq = KEY_BLOCKS_PER_STEP * blk
    row = lax.broadcasted_iota(jnp.int32, (2 * bq, blk), 0) % bq
    col = lax.broadcasted_iota(jnp.int32, (2 * bq, blk), 1)
    if diff:
        lam = _diff_lambda(lam_ref, lam_init)
        subln_w = sub_ref[...]
    else:
        lam = subln_w = None
    neg_inf = jnp.full((2 * bq, LANES), NEG_INF, F32)

    def q_step(i, _):
        q0 = pl.multiple_of(i * bq, bq)
        q2 = _stack_heads(q_ref[0, pl.ds(q0, bq), :])
        if not diff:
            cq = cq_ref[0, 0, pl.ds(q0, bq), :]
            cq2 = jnp.concatenate([cq[:, 0:1], cq[:, 1:2]], axis=0)

        def score(c):
            s = jnp.dot(q2, kt_ref[0, c], preferred_element_type=F32)
            if not diff:
                ck_a = ck_ref[0, 2 * g, pl.ds(c, 1), :]
                ck_b = ck_ref[0, 2 * g + 1, pl.ds(c, 1), :]
                s = jnp.concatenate([s[:bq] + (cq2[:bq] - ck_a), s[bq:] + (cq2[bq:] - ck_b)], axis=0)
            return s

        def pass1(j, m):
            for u in range(KEY_BLOCKS_PER_STEP):
                c = KEY_BLOCKS_PER_STEP * j + u
                s = score(c)
                s_ref[c] = s
                m = jnp.maximum(m, _fold_lanes(s, jnp.maximum))
            return m

        m = lax.fori_loop(0, i, pass1, neg_inf)
        for u in range(KEY_BLOCKS_PER_STEP):
            c = KEY_BLOCKS_PER_STEP * i + u
            kpos = col + u * blk
            allowed = (kpos // CHUNK) <= (row // CHUNK) if diff else kpos <= row
            s = jnp.where(allowed, score(c), NEG_INF)
            s_ref[c] = s
            m = jnp.maximum(m, _fold_lanes(s, jnp.maximum))
        mrow = jnp.max(m, axis=-1, keepdims=True)

        def pass2(j, acc):
            for u in range(KEY_BLOCKS_PER_STEP):
                c = KEY_BLOCKS_PER_STEP * j + u
                p = jnp.exp2(s_ref[c] - mrow).astype(BF16)
                if diff:
                    k0 = pl.multiple_of(c * blk, blk)
                    pv = jnp.dot(p, va_ref[pl.ds(k0, blk), :], preferred_element_type=F32)
                else:
                    pv = lax.dot_general(p, va_ref[c], NT_DIMS, preferred_element_type=F32)
                acc = acc + pv
            return acc

        acc = lax.fori_loop(0, i + 1, pass2, jnp.zeros((2 * bq, 2 * LANES), F32))
        o2 = acc[:, :LANES] / acc[:, LANES:]
        o_ref[0, pl.ds(q0, bq), :] = _merge_heads(o2, bq, diff, lam, subln_w, lam_init)
        return 0

    lax.fori_loop(0, t // bq, q_step, 0)


def _prompt_attention(qh, kth, v_op, batch, *, diff, lam_init=0.0, lam_params=None, subln_w=None,
                      cq=None, ck=None):
    n = qh.shape[1]
    t = n // batch
    blk = kth.shape[3]
    n_blk = t // blk
    tok = pl.BlockSpec((1, t, LANES), lambda g, b: (g, b, 0))
    feat = pl.BlockSpec((1, n_blk, LANES, blk), lambda g, b: (b, 0, g, 0))
    if diff:
        extra_in = [lam_params, subln_w]
        extra_specs = [_const_spec(lam_params.shape), _const_spec(subln_w.shape)]
        v_spec = tok
    else:
        extra_in = [cq, ck]
        extra_specs = [pl.BlockSpec((1, 1, t, 2), lambda g, b: (b, g, 0, 0)),
                       pl.BlockSpec((1, N_SCORE_HEADS, n_blk, blk), lambda g, b: (b, 0, 0, 0))]
        v_spec = feat
    return pl.pallas_call(
        functools.partial(_prompt_attn_body, diff=diff, lam_init=lam_init, blk=blk),
        grid=(N_GROUPS, batch),
        in_specs=[tok, feat, v_spec] + extra_specs,
        out_specs=tok,
        out_shape=jax.ShapeDtypeStruct((N_GROUPS, n, LANES), BF16),
        scratch_shapes=[pltpu.VMEM((n_blk, 2 * KEY_BLOCKS_PER_STEP * blk, blk), F32),
                        pltpu.VMEM((t, 2 * LANES) if diff else (n_blk, 2 * LANES, blk), BF16)],
        compiler_params=_params(),
    )(qh, kth, v_op, *extra_in)


def _sample_attn_body(*refs, diff, lam_init):
    if diff:
        q_ref, ktn_ref, vn_ref, ktc_ref, vc_ref, lam_ref, sub_ref, o_ref = refs
        lam = _diff_lambda(lam_ref, lam_init)
        subln_w = sub_ref[...]
    else:
        q_ref, ktn_ref, vtn_ref, ktc_ref, vtc_ref, cq_ref, cn_ref, cc_ref, o_ref = refs
        lam = subln_w = None
    ts = q_ref.shape[1]
    p_len = ktc_ref.shape[3]
    if not diff:
        row = lax.broadcasted_iota(jnp.int32, (2 * ts, ts), 0) % ts
        col = lax.broadcasted_iota(jnp.int32, (2 * ts, ts), 1)
        causal = col <= row

    def group(g, _):
        r0 = pl.multiple_of(g * LANES, LANES)
        q2 = _stack_heads(q_ref[g])
        ktc = ktc_ref[0, 0, pl.ds(r0, LANES), :].astype(BF16)
        sc = jnp.dot(q2, ktc, preferred_element_type=F32)
        sn = jnp.dot(q2, ktn_ref[0, pl.ds(r0, LANES), :], preferred_element_type=F32)
        if not diff:
            cq = cq_ref[0, g]
            cq2 = jnp.concatenate([cq[:, 0:1], cq[:, 1:2]], axis=0)
            cc_a = cc_ref[0, pl.ds(2 * g, 1), :]
            cc_b = cc_ref[0, pl.ds(2 * g + 1, 1), :]
            cn_a = cn_ref[0, pl.ds(2 * g, 1), :]
            cn_b = cn_ref[0, pl.ds(2 * g + 1, 1), :]
            sc = jnp.concatenate(
                [sc[:ts] + ((cq2[:ts] + cc_a[:, p_len - 1:p_len]) - cc_a),
                 sc[ts:] + ((cq2[ts:] + cc_b[:, p_len - 1:p_len]) - cc_b)], axis=0)
            sn = jnp.concatenate([sn[:ts] + (cq2[:ts] - cn_a), sn[ts:] + (cq2[ts:] - cn_b)], axis=0)
            sn = jnp.where(causal, sn, NEG_INF)
        m = jnp.maximum(jnp.max(sc, axis=-1, keepdims=True), jnp.max(sn, axis=-1, keepdims=True))
        pc = jnp.exp2(sc - m)
        pn = jnp.exp2(sn - m)
        l = jnp.sum(pc, axis=-1, keepdims=True) + jnp.sum(pn, axis=-1, keepdims=True)
        if diff:
            vc = vc_ref[0, 0, pl.ds(g, p_len, stride=N_GROUPS), :].astype(BF16)
            acc = (jnp.dot(pc.astype(BF16), vc, preferred_element_type=F32)
                   + jnp.dot(pn.astype(BF16), vn_ref[g], preferred_element_type=F32))
        else:
            vtc = vtc_ref[0, 0, pl.ds(r0, LANES), :].astype(BF16)
            acc = (lax.dot_general(pc.astype(BF16), vtc, NT_DIMS, preferred_element_type=F32)
                   + lax.dot_general(pn.astype(BF16), vtn_ref[0, pl.ds(r0, LANES), :], NT_DIMS,
                                     preferred_element_type=F32))
        o_ref[g] = _merge_heads(acc / l, ts, diff, lam, subln_w, lam_init)
        return 0

    lax.fori_loop(0, N_GROUPS, group, 0)


def _sample_attention(qh, ktn, v_new, cache_kt, cache_v, layer, *, diff, lam_init=0.0, lam_params=None,
                      subln_w=None, cq=None, cn=None, cc=None):
    _, batch, _, p_len = cache_kt.shape
    n = qh.shape[1]
    ts = n // batch
    tok = pl.BlockSpec((N_GROUPS, ts, LANES), lambda b: (0, b, 0))
    feat = pl.BlockSpec((1, D_MODEL, ts), lambda b: (b, 0, 0))
    cache_feat = pl.BlockSpec((1, 1, D_MODEL, p_len), lambda b: (layer, b, 0, 0))
    if diff:
        extra_in = [lam_params, subln_w]
        extra_specs = [_const_spec(lam_params.shape), _const_spec(subln_w.shape)]
        v_specs = [tok, cache_feat, pl.BlockSpec((1, 1, p_len * N_GROUPS, LANES), lambda b: (layer, b, 0, 0))]
    else:
        extra_in = [cq, cn, cc]
        extra_specs = [pl.BlockSpec((1, N_GROUPS, ts, 2), lambda b: (b, 0, 0, 0)),
                       pl.BlockSpec((1, N_SCORE_HEADS, ts), lambda b: (b, 0, 0)),
                       pl.BlockSpec((1, N_SCORE_HEADS, p_len), lambda b: (b, 0, 0))]
        v_specs = [feat, cache_feat, cache_feat]
    return pl.pallas_call(
        functools.partial(_sample_attn_body, diff=diff, lam_init=lam_init),
        grid=(batch,),
        in_specs=[tok, feat] + v_specs + extra_specs,
        out_specs=tok,
        out_shape=jax.ShapeDtypeStruct((N_GROUPS, n, LANES), BF16),
        compiler_params=_params(),
    )(qh, ktn, v_new, cache_kt, cache_v, *extra_in)


def _mlp_body(x_ref, o_ref, wo_ref, nw_ref, win_ref, wout_ref, nf_ref, y_ref, *, final):
    d_ff = wout_ref.shape[0]
    o = jnp.concatenate([o_ref[g] for g in range(N_GROUPS)], axis=1)
    x1 = x_ref[...] + jnp.dot(o, wo_ref[...], preferred_element_type=F32)
    h = _rmsnorm_rows(x1, nw_ref[...], NORM_EPS).astype(BF16)
    gate = jnp.dot(h, win_ref[:, :d_ff], preferred_element_type=F32)
    up = jnp.dot(h, win_ref[:, d_ff:], preferred_element_type=F32)
    act = (gate * jax.nn.sigmoid(gate) * up).astype(BF16)
    x2 = x1 + jnp.dot(act, wout_ref[...], preferred_element_type=F32)
    if final:
        x2 = _rmsnorm_rows(x2, nf_ref[...], NORM_EPS)
    y_ref[...] = x2


def _mlp(x2d, o_hm, w_o, norm_w, w_in, w_out, norm_final, *, final):
    n = x2d.shape[0]
    tm = min(ROW_BLOCK, n)
    d_ff = w_out.shape[0]
    return pl.pallas_call(
        functools.partial(_mlp_body, final=final),
        grid=(n // tm,),
        in_specs=[pl.BlockSpec((tm, D_MODEL), lambda i: (i, 0)),
                  pl.BlockSpec((N_GROUPS, tm, LANES), lambda i: (0, i, 0)),
                  _const_spec((D_MODEL, D_MODEL)),
                  _const_spec((1, D_MODEL)),
                  _const_spec((D_MODEL, 2 * d_ff)),
                  _const_spec((d_ff, D_MODEL)),
                  _const_spec((1, D_MODEL))],
        out_specs=pl.BlockSpec((tm, D_MODEL), lambda i: (i, 0)),
        out_shape=jax.ShapeDtypeStruct((n, D_MODEL), F32),
        compiler_params=_params(),
    )(x2d, o_hm, w_o, norm_w, w_in, w_out, norm_final)


def _rope_tables(pos):
    half = HEAD_DIM // 2
    inv_freq = ROPE_THETA ** (-jnp.arange(half, dtype=F32) / half)
    ang = pos.astype(F32)[:, None] * inv_freq[None, :]
    cos = jnp.cos(ang)
    sin = jnp.sin(ang)
    cos_tok = jnp.tile(cos, (1, LANES // half))
    sin_tok = jnp.tile(jnp.concatenate([-sin, sin], axis=1), (1, LANES // HEAD_DIM))
    return cos_tok, sin_tok, cos.T, sin.T


def _lambda_init(layer):
    return 0.8 - 0.6 * math.exp(-0.3 * layer)


def _pair_columns(c_t):
    b, _, t = c_t.shape
    return jnp.swapaxes(c_t.reshape(b, N_GROUPS, 2, t), 2, 3)


def _feature_major_to_tokens(a, heads):
    *lead, b, _, t = a.shape
    a = a.reshape(*lead, b, heads, D_MODEL // heads, t)
    return jnp.moveaxis(a, -1, -3)


def kernel(x_prompt, x_sample, cache_diff_k, cache_diff_v, cache_fox_k, cache_fox_v, cache_fox_logf,
           norm_mix, norm_ffn, norm_final, diff_w_qkv, diff_w_o, diff_lambda, diff_subln,
           fox_w_qkvf, fox_b_f, fox_w_o, ffn_w_in, ffn_w_out):
    bp, s_len, _ = x_prompt.shape
    bs, t_len, _ = x_sample.shape
    p_len = cache_diff_k.shape[2]
    depth = norm_mix.shape[0]
    n_diff = diff_w_qkv.shape[0]
    n_fox = fox_w_qkvf.shape[0]
    dv_heads = cache_diff_v.shape[3]

    ns = bs * t_len
    tm_s = min(ROW_BLOCK, ns)
    xp3 = x_prompt
    xs3 = x_sample.reshape(ns // tm_s, tm_s, D_MODEL)
    xp = x_prompt.reshape(bp * s_len, D_MODEL)
    xs = x_sample.reshape(ns, D_MODEL)

    tab_p = _rope_tables(jnp.arange(s_len, dtype=jnp.int32))
    tab_s = _rope_tables(p_len + jnp.arange(t_len, dtype=jnp.int32))
    reps = tm_s // t_len
    tab_s = (jnp.tile(tab_s[0], (reps, 1)), jnp.tile(tab_s[1], (reps, 1)),
             jnp.tile(tab_s[2], (1, reps)), jnp.tile(tab_s[3], (1, reps)))

    q_scale = HEAD_DIM ** -0.5 * LOG2_E
    d = D_MODEL
    wd = [((diff_w_qkv[j, :, :d] * q_scale).astype(BF16),
           diff_w_qkv[j, :, d:2 * d].T.astype(BF16),
           diff_w_qkv[j, :, 2 * d:].astype(BF16)) for j in range(n_diff)]
    pad = LANES - N_SCORE_HEADS
    wf = [((fox_w_qkvf[j, :, :d] * q_scale).astype(BF16),
           fox_w_qkvf[j, :, d:2 * d].T.astype(BF16),
           fox_w_qkvf[j, :, 2 * d:3 * d].T.astype(BF16),
           jnp.concatenate([fox_w_qkvf[j, :, 3 * d:].T, jnp.zeros((pad, d), F32)], axis=0).astype(BF16))
          for j in range(n_fox)]
    b_f = [jnp.concatenate([fox_b_f[j], jnp.zeros((pad,), F32)]).reshape(LANES, 1) for j in range(n_fox)]
    w_o_d = diff_w_o.astype(BF16)
    w_o_f = fox_w_o.astype(BF16)
    w_in = ffn_w_in.astype(BF16)
    w_out = ffn_w_out.astype(BF16)
    norm_mix = norm_mix.reshape(depth, 1, D_MODEL)
    norm_ffn = norm_ffn.reshape(depth, 1, D_MODEL)
    norm_final = norm_final.reshape(1, D_MODEL)

    def feature_major(c):
        l, b, p, hh, dd = c.shape
        return jnp.transpose(c, (0, 1, 3, 4, 2)).reshape(l, b, hh * dd, p)

    ckd = feature_major(cache_diff_k)
    cvd = cache_diff_v.reshape(n_diff, bs, p_len * dv_heads, LANES)
    ckf = feature_major(cache_fox_k)
    cvf = feature_major(cache_fox_v)

    def sample_feature_major(a):
        a = a.reshape(-1, D_MODEL, tm_s // t_len, t_len)
        return jnp.transpose(a, (0, 2, 1, 3)).reshape(bs, D_MODEL, t_len)

    prev_d = prev_f = None
    samp = {k: [] for k in ("dk", "dv", "fk", "fv", "fl")}
    for i in range(depth):
        j = i // 2
        final = i == depth - 1
        if i % 2 == 0:
            lam_init = _lambda_init(i)
            sub = diff_subln[j].reshape(1, LANES)
            qh, kth, vh, kt32, v32 = _project(xp3, norm_mix[i], wd[j], rotary=True, tables=tab_p,
                                              layer=j, n_layers=n_diff, prev=prev_d)
            prev_d = (kt32, v32)
            op = _prompt_attention(qh, kth, vh, bp, diff=True, lam_init=lam_init,
                                   lam_params=diff_lambda[j], subln_w=sub)
            qh, kth, vh, kt32, v32 = _project(xs3, norm_mix[i], wd[j], rotary=True, tables=tab_s)
            os_ = _sample_attention(qh, sample_feature_major(kth), vh, ckd, cvd, j, diff=True,
                                    lam_init=lam_init, lam_params=diff_lambda[j], subln_w=sub)
            samp["dk"].append(_feature_major_to_tokens(sample_feature_major(kt32[0]), N_SCORE_HEADS))
            samp["dv"].append(v32[0].reshape(bs, t_len, dv_heads, LANES))
            w_o = w_o_d[j]
        else:
            qh, kth, vth, kt32, vt32, lft = _project(xp3, norm_mix[i], wf[j], rotary=False, b_f=b_f[j],
                                                    layer=j, n_layers=n_fox, prev=prev_f)
            prev_f = (kt32, vt32, lft)
            c_t = _cumsum_time(lft[j], min(512, s_len))
            ck = c_t.reshape(bp, N_SCORE_HEADS, kth.shape[1], kth.shape[3])
            op = _prompt_attention(qh, kth, vth, bp, diff=False, cq=_pair_columns(c_t), ck=ck)
            qh, kth, vth, kt32, vt32, lft_s = _project(xs3, norm_mix[i], wf[j], rotary=False, b_f=b_f[j])
            lf_s = jnp.transpose(lft_s[0].reshape(ns // tm_s, N_SCORE_HEADS, tm_s // t_len, t_len),
                                 (0, 2, 1, 3)).reshape(bs, N_SCORE_HEADS, t_len)
            cc = _cumsum_time(jnp.swapaxes(cache_fox_logf[j], 1, 2), min(512, p_len))
            cn = _cumsum_time(lf_s, t_len)
            os_ = _sample_attention(qh, sample_feature_major(kth), sample_feature_major(vth), ckf, cvf, j,
                                    diff=False, cq=_pair_columns(cn), cn=cn, cc=cc)
            samp["fk"].append(_feature_major_to_tokens(sample_feature_major(kt32[0]), N_SCORE_HEADS))
            samp["fv"].append(_feature_major_to_tokens(sample_feature_major(vt32[0]), N_SCORE_HEADS))
            samp["fl"].append(jnp.swapaxes(lf_s, 1, 2))
            w_o = w_o_f[j]
        xp = _mlp(xp, op, w_o, norm_ffn[i], w_in[i], w_out[i], norm_final, final=final)
        xs = _mlp(xs, os_, w_o, norm_ffn[i], w_in[i], w_out[i], norm_final, final=final)
        xp3 = xp.reshape(bp, s_len, D_MODEL)
        xs3 = xs.reshape(ns // tm_s, tm_s, D_MODEL)

    dk32, dv32 = prev_d
    fk32, fv32, fl32 = prev_f
    return (xp.reshape(bp, s_len, D_MODEL), xs.reshape(bs, t_len, D_MODEL),
            _feature_major_to_tokens(dk32, N_SCORE_HEADS),
            dv32.reshape(n_diff, bp, s_len, dv_heads, LANES),
            _feature_major_to_tokens(fk32, N_SCORE_HEADS),
            _feature_major_to_tokens(fv32, N_SCORE_HEADS),
            jnp.swapaxes(fl32, 2, 3),
            jnp.stack(samp["dk"]), jnp.stack(samp["dv"]), jnp.stack(samp["fk"]),
            jnp.stack(samp["fv"]), jnp.stack(samp["fl"]))
```

```python
import functools
import math

import jax
import jax.numpy as jnp
from jax import lax
from jax.experimental import pallas as pl
from jax.experimental.pallas import tpu as pltpu

D_MODEL = 1024
CHUNK = 64
HEAD_DIM = 64
N_SCORE_HEADS = 16
ROPE_THETA = 10000.0
NORM_EPS = 1e-6
SUBLN_EPS = 1e-5

LANES = 128
N_GROUPS = D_MODEL // LANES
VMEM_LIMIT_BYTES = 56 * 1024 * 1024
ROW_BLOCK = 512
KEY_BLOCKS_PER_STEP = 2

F32 = jnp.float32
BF16 = jnp.bfloat16
NEG_INF = float("-inf")
LOG2_E = math.log2(math.e)
NT_DIMS = (((1,), (1,)), ((), ()))


def _params():
    return pltpu.CompilerParams(vmem_limit_bytes=VMEM_LIMIT_BYTES)


def _const_spec(shape):
    nd = len(shape)
    return pl.BlockSpec(shape, lambda *_: (0,) * nd, pipeline_mode=pl.Buffered(1))


def _rmsnorm_rows(x, w, eps):
    ms = jnp.mean(x * x, axis=-1, keepdims=True)
    return x * lax.rsqrt(ms + eps) * w


def _proj_body(*refs, rotary, n_prev):
    if rotary:
        (x_ref, nw_ref, wq_ref, wkt_ref, wv_ref, cos_ref, sin_ref, cost_ref, sint_ref) = refs[:9]
        qh_ref, kth_ref, vh_ref, kt_ref, v_ref = refs[9 + n_prev:]
    else:
        (x_ref, nw_ref, wq_ref, wkt_ref, wvt_ref, wft_ref, bf_ref) = refs[:7]
        qh_ref, kth_ref, vth_ref, kt_ref, vt_ref, lft_ref = refs[7 + n_prev:]
    tm = x_ref.shape[1]
    h = _rmsnorm_rows(x_ref[0], nw_ref[...], NORM_EPS).astype(BF16)
    q = jnp.dot(h, wq_ref[...], preferred_element_type=F32)
    kt = lax.dot_general(wkt_ref[...], h, NT_DIMS, preferred_element_type=F32)
    half = HEAD_DIM // 2
    if rotary:
        cs = cos_ref[...]
        sn = sin_ref[...]
        lane = lax.broadcasted_iota(jnp.int32, cs.shape, 1)
        first_half = (lane & half) == 0
        for g in range(N_GROUPS):
            qg = q[:, LANES * g:LANES * (g + 1)]
            qp = jnp.where(first_half, pltpu.roll(qg, LANES - half, 1), pltpu.roll(qg, half, 1))
            qh_ref[g] = (qg * cs + qp * sn).astype(BF16)
        ct = cost_ref[...]
        st = sint_ref[...]
        for hh in range(N_SCORE_HEADS):
            r0 = HEAD_DIM * hh
            a = kt[r0:r0 + half]
            b = kt[r0 + half:r0 + HEAD_DIM]
            ka = a * ct - b * st
            kb = b * ct + a * st
            kt_ref[0, 0, r0:r0 + half, :] = ka
            kt_ref[0, 0, r0 + half:r0 + HEAD_DIM, :] = kb
            kth_ref[0, 0, r0:r0 + half, :] = ka.astype(BF16)
            kth_ref[0, 0, r0 + half:r0 + HEAD_DIM, :] = kb.astype(BF16)
        v = jnp.dot(h, wv_ref[...], preferred_element_type=F32)
        for g in range(N_GROUPS):
            vg = v[:, LANES * g:LANES * (g + 1)]
            vh_ref[g] = vg.astype(BF16)
            v_ref[0, pl.ds(g, tm, stride=N_GROUPS), :] = vg
    else:
        for g in range(N_GROUPS):
            qh_ref[g] = q[:, LANES * g:LANES * (g + 1)].astype(BF16)
        kt_ref[0, 0] = kt
        kth_ref[0, 0] = kt.astype(BF16)
        vt = lax.dot_general(wvt_ref[...], h, NT_DIMS, preferred_element_type=F32)
        vt_ref[0, 0] = vt
        vth_ref[0, 0] = vt.astype(BF16)
        z = lax.dot_general(wft_ref[...], h, NT_DIMS, preferred_element_type=F32) + bf_ref[...]
        lf = jnp.minimum(z, 0.0) - jnp.log1p(jnp.exp(-jnp.abs(z)))
        lft_ref[0, 0] = lf[:N_SCORE_HEADS]


def _project(x3, norm_w, weights, *, rotary, tables=None, b_f=None, layer=0, n_layers=1, prev=None):
    bx, tx, _ = x3.shape
    tm = min(ROW_BLOCK, tx)
    n = bx * tx
    t_blocks = tx // tm
    grid = (t_blocks, bx)
    prev = list(prev) if prev is not None else []
    n_in_fixed = 9 if rotary else 7
    xspec = pl.BlockSpec((1, tm, D_MODEL), lambda t, b: (b, t, 0))
    wspecs = [_const_spec(w.shape) for w in weights]
    hm = jax.ShapeDtypeStruct((N_GROUPS, n, LANES), BF16)
    hm_spec = pl.BlockSpec((N_GROUPS, tm, LANES), lambda t, b: (0, b * t_blocks + t, 0))
    fm16 = jax.ShapeDtypeStruct((bx, t_blocks, D_MODEL, tm), BF16)
    fm16_spec = pl.BlockSpec((1, 1, D_MODEL, tm), lambda t, b: (b, t, 0, 0))
    fm32 = jax.ShapeDtypeStruct((n_layers, bx, D_MODEL, tx), F32)
    fm32_spec = pl.BlockSpec((1, 1, D_MODEL, tm), lambda t, b: (layer, b, 0, t))
    if rotary:
        cos, sin, cos_t, sin_t = tables
        extra_in = [cos, sin, cos_t, sin_t]
        extra_specs = [pl.BlockSpec((tm, LANES), lambda t, b: (t, 0)),
                       pl.BlockSpec((tm, LANES), lambda t, b: (t, 0)),
                       pl.BlockSpec((HEAD_DIM // 2, tm), lambda t, b: (0, t)),
                       pl.BlockSpec((HEAD_DIM // 2, tm), lambda t, b: (0, t))]
        out_shape = [hm, fm16, hm, fm32,
                     jax.ShapeDtypeStruct((n_layers, n * N_GROUPS, LANES), F32)]
        out_specs = [hm_spec, fm16_spec, hm_spec, fm32_spec,
                     pl.BlockSpec((1, tm * N_GROUPS, LANES), lambda t, b: (layer, b * t_blocks + t, 0))]
        n_f32 = 2
    else:
        extra_in = [b_f]
        extra_specs = [_const_spec(b_f.shape)]
        out_shape = [hm, fm16, fm16, fm32, fm32,
                     jax.ShapeDtypeStruct((n_layers, bx, N_SCORE_HEADS, tx), F32)]
        out_specs = [hm_spec, fm16_spec, fm16_spec, fm32_spec, fm32_spec,
                     pl.BlockSpec((1, 1, N_SCORE_HEADS, tm), lambda t, b: (layer, b, 0, t))]
        n_f32 = 3
    assert len(prev) in (0, n_f32)
    aliases = {n_in_fixed + k: 3 + k for k in range(len(prev))}
    return pl.pallas_call(
        functools.partial(_proj_body, rotary=rotary, n_prev=len(prev)),
        grid=grid,
        in_specs=[xspec, _const_spec((1, D_MODEL))] + wspecs + extra_specs
                 + [pl.BlockSpec(memory_space=pl.ANY)] * len(prev),
        out_specs=out_specs,
        out_shape=out_shape,
        input_output_aliases=aliases,
        compiler_params=_params(),
    )(x3, norm_w, *weights, *extra_in, *prev)


def _cumsum_body(x_ref, o_ref, *, chunk):
    t = x_ref.shape[2]
    r = lax.broadcasted_iota(jnp.int32, (chunk, chunk), 0)
    c = lax.broadcasted_iota(jnp.int32, (chunk, chunk), 1)
    upper = (r <= c).astype(BF16)
    carry = jnp.zeros((x_ref.shape[1], 1), F32)
    for i in range(t // chunk):
        x = x_ref[0, :, chunk * i:chunk * (i + 1)]
        hi = x.astype(BF16)
        r1 = x - hi.astype(F32)
        mid = r1.astype(BF16)
        lo = (r1 - mid.astype(F32)).astype(BF16)
        cs = (jnp.dot(hi, upper, preferred_element_type=F32)
              + jnp.dot(mid, upper, preferred_element_type=F32)
              + jnp.dot(lo, upper, preferred_element_type=F32)) + carry
        o_ref[0, :, chunk * i:chunk * (i + 1)] = cs * LOG2_E
        carry = cs[:, chunk - 1:chunk]


def _cumsum_time(lf_t, chunk):
    b, h, t = lf_t.shape
    return pl.pallas_call(
        functools.partial(_cumsum_body, chunk=chunk),
        grid=(b,),
        in_specs=[pl.BlockSpec((1, h, t), lambda i: (i, 0, 0))],
        out_specs=pl.BlockSpec((1, h, t), lambda i: (i, 0, 0)),
        out_shape=jax.ShapeDtypeStruct((b, h, t), F32),
        compiler_params=_params(),
    )(lf_t)


def _diff_lambda(lam_ref, lam_init):
    lp = lam_ref[...]
    a = jnp.sum(lp[0:1] * lp[1:2], axis=-1, keepdims=True)
    b = jnp.sum(lp[2:3] * lp[3:4], axis=-1, keepdims=True)
    return jnp.exp(a) - jnp.exp(b) + lam_init


def _diff_combine(o1, o2, lam, subln_w, lam_init):
    o = o1 - lam * o2
    return _rmsnorm_rows(o, subln_w, SUBLN_EPS) * (1.0 - lam_init)


def _stack_heads(qb):
    low = lax.broadcasted_iota(jnp.int32, qb.shape, 1) < HEAD_DIM
    zero = jnp.zeros_like(qb)
    return jnp.concatenate([jnp.where(low, qb, zero), jnp.where(low, zero, qb)], axis=0)


def _head_rows(x, t, r0):
    if r0 == 0:
        return x
    return jnp.concatenate([x[r0:t], x[t + r0:]], axis=0)


def _update_head_rows(x, t, r0, y, op):
    if r0 == 0:
        return op(x, y)
    h = t - r0
    return jnp.concatenate([x[:r0], op(x[r0:t], y[:h]), x[t:t + r0], op(x[t + r0:], y[h:])], axis=0)


def _fold_lanes(x, op):
    acc = x[:, :LANES]
    for c in range(1, x.shape[1] // LANES):
        acc = op(acc, x[:, LANES * c:LANES * (c + 1)])
    return acc


def _merge_heads(o2, t, diff, lam, subln_w, lam_init):
    o_a, o_b = o2[:t], o2[t:]
    if diff:
        out = _diff_combine(o_a, o_b, lam, subln_w, lam_init)
    else:
        low = lax.broadcasted_iota(jnp.int32, o_a.shape, 1) < HEAD_DIM
        out = jnp.where(low, o_a, o_b)
    return out.astype(BF16)


def _prompt_attn_body(*refs, diff, lam_init, blk):
    if diff:
        q_ref, kt_ref, v_ref, lam_ref, sub_ref, o_ref, s_ref, va_ref = refs
        va_ref[:, :LANES] = v_ref[0]
        va_ref[:, LANES:] = jnp.ones((va_ref.shape[0], LANES), BF16)
    else:
        q_ref, kt_ref, vt_ref, cq_ref, ck_ref, o_ref, s_ref, va_ref = refs
        va_ref[:, :LANES, :] = vt_ref[0]
        va_ref[:, LANES:, :] = jnp.ones((va_ref.shape[0], LANES, va_ref.shape[2]), BF16)
    t = q_ref.shape[1]
    g = pl.program_id(0)
    bq = KEY_BLOCKS_PER_STEP * blk
    if diff:
        lam = _diff_lambda(lam_ref, lam_init)
        subln_w = sub_ref[...]
    else:
        lam = subln_w = None
    neg_inf = jnp.full((2 * bq, LANES), NEG_INF, F32)

    def q_step(i, _):
        q0 = pl.multiple_of(i * bq, bq)
        q2 = _stack_heads(q_ref[0, pl.ds(q0, bq), :])
        if not diff:
            cq = cq_ref[0, 0, pl.ds(q0, bq), :]
            cq2 = jnp.concatenate([cq[:, 0:1], cq[:, 1:2]], axis=0)

        def score(c, r0=0):
            q_sub = _head_rows(q2, bq, r0)
            s = jnp.dot(q_sub, kt_ref[0, c], preferred_element_type=F32)
            if not diff:
                h = bq - r0
                ck_a = ck_ref[0, 2 * g, pl.ds(c, 1), :]
                ck_b = ck_ref[0, 2 * g + 1, pl.ds(c, 1), :]
                s = jnp.concatenate([s[:h] + (cq2[r0:bq] - ck_a), s[h:] + (cq2[bq + r0:] - ck_b)], axis=0)
            return s

        def weigh(c, mrow_sub, r0=0):
            h = bq - r0
            if r0 == 0:
                sc = s_ref[c]
            else:
                sc = jnp.concatenate([s_ref[c, r0:bq, :], s_ref[c, bq + r0:, :]], axis=0)
            p = jnp.exp2(sc - mrow_sub).astype(BF16)
            if diff:
                k0 = pl.multiple_of(c * blk, blk)
                return jnp.dot(p, va_ref[pl.ds(k0, blk), :], preferred_element_type=F32)
            return lax.dot_general(p, va_ref[c], NT_DIMS, preferred_element_type=F32)

        def pass1(j, m):
            for u in range(KEY_BLOCKS_PER_STEP):
                c = KEY_BLOCKS_PER_STEP * j + u
                s = score(c)
                s_ref[c] = s
                m = jnp.maximum(m, _fold_lanes(s, jnp.maximum))
            return m

        m = lax.fori_loop(0, i, pass1, neg_inf)
        for u in range(KEY_BLOCKS_PER_STEP):
            c = KEY_BLOCKS_PER_STEP * i + u
            r0 = u * blk
            h = bq - r0
            qpos = r0 + lax.broadcasted_iota(jnp.int32, (2 * h, blk), 0) % h
            kpos = r0 + lax.broadcasted_iota(jnp.int32, (2 * h, blk), 1)
            allowed = (kpos // CHUNK) <= (qpos // CHUNK) if diff else kpos <= qpos
            s = jnp.where(allowed, score(c, r0), NEG_INF)
            s_ref[c, r0:bq, :] = s[:h]
            s_ref[c, bq + r0:, :] = s[h:]
            m = _update_head_rows(m, bq, r0, _fold_lanes(s, jnp.maximum), jnp.maximum)
        mrow = jnp.max(m, axis=-1, keepdims=True)

        def pass2(j, acc):
            for u in range(KEY_BLOCKS_PER_STEP):
                acc = acc + weigh(KEY_BLOCKS_PER_STEP * j + u, mrow)
            return acc

        acc = lax.fori_loop(0, i, pass2, jnp.zeros((2 * bq, 2 * LANES), F32))
        for u in range(KEY_BLOCKS_PER_STEP):
            r0 = u * blk
            pv = weigh(KEY_BLOCKS_PER_STEP * i + u, _head_rows(mrow, bq, r0), r0)
            acc = _update_head_rows(acc, bq, r0, pv, jnp.add)
        o2 = acc[:, :LANES] / acc[:, LANES:]
        o_ref[0, pl.ds(q0, bq), :] = _merge_heads(o2, bq, diff, lam, subln_w, lam_init)
        return 0

    lax.fori_loop(0, t // bq, q_step, 0)


def _prompt_attention(qh, kth, v_op, batch, *, diff, lam_init=0.0, lam_params=None, subln_w=None,
                      cq=None, ck=None):
    n = qh.shape[1]
    t = n // batch
    blk = kth.shape[3]
    n_blk = t // blk
    tok = pl.BlockSpec((1, t, LANES), lambda g, b: (g, b, 0))
    feat = pl.BlockSpec((1, n_blk, LANES, blk), lambda g, b: (b, 0, g, 0))
    if diff:
        extra_in = [lam_params, subln_w]
        extra_specs = [_const_spec(lam_params.shape), _const_spec(subln_w.shape)]
        v_spec = tok
    else:
        extra_in = [cq, ck]
        extra_specs = [pl.BlockSpec((1, 1, t, 2), lambda g, b: (b, g, 0, 0)),
                       pl.BlockSpec((1, N_SCORE_HEADS, n_blk, blk), lambda g, b: (b, 0, 0, 0))]
        v_spec = feat
    return pl.pallas_call(
        functools.partial(_prompt_attn_body, diff=diff, lam_init=lam_init, blk=blk),
        grid=(N_GROUPS, batch),
        in_specs=[tok, feat, v_spec] + extra_specs,
        out_specs=tok,
        out_shape=jax.ShapeDtypeStruct((N_GROUPS, n, LANES), BF16),
        scratch_shapes=[pltpu.VMEM((n_blk, 2 * KEY_BLOCKS_PER_STEP * blk, blk), F32),
                        pltpu.VMEM((t, 2 * LANES) if diff else (n_blk, 2 * LANES, blk), BF16)],
        compiler_params=_params(),
    )(qh, kth, v_op, *extra_in)


def _sample_attn_body(*refs, diff, lam_init):
    if diff:
        q_ref, ktn_ref, vn_ref, ktc_ref, vc_ref, lam_ref, sub_ref, o_ref = refs
        lam = _diff_lambda(lam_ref, lam_init)
        subln_w = sub_ref[...]
    else:
        q_ref, ktn_ref, vtn_ref, ktc_ref, vtc_ref, cq_ref, cn_ref, cc_ref, o_ref = refs
        lam = subln_w = None
    ts = q_ref.shape[1]
    p_len = ktc_ref.shape[3]
    if not diff:
        row = lax.broadcasted_iota(jnp.int32, (2 * ts, ts), 0) % ts
        col = lax.broadcasted_iota(jnp.int32, (2 * ts, ts), 1)
        causal = col <= row

    def group(g, _):
        r0 = pl.multiple_of(g * LANES, LANES)
        q2 = _stack_heads(q_ref[g])
        ktc = ktc_ref[0, 0, pl.ds(r0, LANES), :].astype(BF16)
        sc = jnp.dot(q2, ktc, preferred_element_type=F32)
        sn = jnp.dot(q2, ktn_ref[0, pl.ds(r0, LANES), :], preferred_element_type=F32)
        if not diff:
            cq = cq_ref[0, g]
            cq2 = jnp.concatenate([cq[:, 0:1], cq[:, 1:2]], axis=0)
            cc_a = cc_ref[0, pl.ds(2 * g, 1), :]
            cc_b = cc_ref[0, pl.ds(2 * g + 1, 1), :]
            cn_a = cn_ref[0, pl.ds(2 * g, 1), :]
            cn_b = cn_ref[0, pl.ds(2 * g + 1, 1), :]
            sc = jnp.concatenate(
                [sc[:ts] + ((cq2[:ts] + cc_a[:, p_len - 1:p_len]) - cc_a),
                 sc[ts:] + ((cq2[ts:] + cc_b[:, p_len - 1:p_len]) - cc_b)], axis=0)
            sn = jnp.concatenate([sn[:ts] + (cq2[:ts] - cn_a), sn[ts:] + (cq2[ts:] - cn_b)], axis=0)
            sn = jnp.where(causal, sn, NEG_INF)
        m = jnp.maximum(jnp.max(sc, axis=-1, keepdims=True), jnp.max(sn, axis=-1, keepdims=True))
        pc = jnp.exp2(sc - m)
        pn = jnp.exp2(sn - m)
        l = jnp.sum(pc, axis=-1, keepdims=True) + jnp.sum(pn, axis=-1, keepdims=True)
        if diff:
            vc = vc_ref[0, 0, pl.ds(g, p_len, stride=N_GROUPS), :].astype(BF16)
            acc = (jnp.dot(pc.astype(BF16), vc, preferred_element_type=F32)
                   + jnp.dot(pn.astype(BF16), vn_ref[g], preferred_element_type=F32))
        else:
            vtc = vtc_ref[0, 0, pl.ds(r0, LANES), :].astype(BF16)
            acc = (lax.dot_general(pc.astype(BF16), vtc, NT_DIMS, preferred_element_type=F32)
                   + lax.dot_general(pn.astype(BF16), vtn_ref[0, pl.ds(r0, LANES), :], NT_DIMS,
                                     preferred_element_type=F32))
        o_ref[g] = _merge_heads(acc / l, ts, diff, lam, subln_w, lam_init)
        return 0

    lax.fori_loop(0, N_GROUPS, group, 0)


def _sample_attention(qh, ktn, v_new, cache_kt, cache_v, layer, *, diff, lam_init=0.0, lam_params=None,
                      subln_w=None, cq=None, cn=None, cc=None):
    _, batch, _, p_len = cache_kt.shape
    n = qh.shape[1]
    ts = n // batch
    tok = pl.BlockSpec((N_GROUPS, ts, LANES), lambda b: (0, b, 0))
    feat = pl.BlockSpec((1, D_MODEL, ts), lambda b: (b, 0, 0))
    cache_feat = pl.BlockSpec((1, 1, D_MODEL, p_len), lambda b: (layer, b, 0, 0))
    if diff:
        extra_in = [lam_params, subln_w]
        extra_specs = [_const_spec(lam_params.shape), _const_spec(subln_w.shape)]
        v_specs = [tok, cache_feat, pl.BlockSpec((1, 1, p_len * N_GROUPS, LANES), lambda b: (layer, b, 0, 0))]
    else:
        extra_in = [cq, cn, cc]
        extra_specs = [pl.BlockSpec((1, N_GROUPS, ts, 2), lambda b: (b, 0, 0, 0)),
                       pl.BlockSpec((1, N_SCORE_HEADS, ts), lambda b: (b, 0, 0)),
                       pl.BlockSpec((1, N_SCORE_HEADS, p_len), lambda b: (b, 0, 0))]
        v_specs = [feat, cache_feat, cache_feat]
    return pl.pallas_call(
        functools.partial(_sample_attn_body, diff=diff, lam_init=lam_init),
        grid=(batch,),
        in_specs=[tok, feat] + v_specs + extra_specs,
        out_specs=tok,
        out_shape=jax.ShapeDtypeStruct((N_GROUPS, n, LANES), BF16),
        compiler_params=_params(),
    )(qh, ktn, v_new, cache_kt, cache_v, *extra_in)


def _mlp_body(x_ref, o_ref, wo_ref, nw_ref, win_ref, wout_ref, nf_ref, y_ref, *, final):
    d_ff = wout_ref.shape[0]
    o = jnp.concatenate([o_ref[g] for g in range(N_GROUPS)], axis=1)
    x1 = x_ref[...] + jnp.dot(o, wo_ref[...], preferred_element_type=F32)
    h = _rmsnorm_rows(x1, nw_ref[...], NORM_EPS).astype(BF16)
    gate = jnp.dot(h, win_ref[:, :d_ff], preferred_element_type=F32)
    up = jnp.dot(h, win_ref[:, d_ff:], preferred_element_type=F32)
    act = (gate * jax.nn.sigmoid(gate) * up).astype(BF16)
    x2 = x1 + jnp.dot(act, wout_ref[...], preferred_element_type=F32)
    if final:
        x2 = _rmsnorm_rows(x2, nf_ref[...], NORM_EPS)
    y_ref[...] = x2


def _mlp(x2d, o_hm, w_o, norm_w, w_in, w_out, norm_final, *, final):
    n = x2d.shape[0]
    tm = min(ROW_BLOCK, n)
    d_ff = w_out.shape[0]
    return pl.pallas_call(
        functools.partial(_mlp_body, final=final),
        grid=(n // tm,),
        in_specs=[pl.BlockSpec((tm, D_MODEL), lambda i: (i, 0)),
                  pl.BlockSpec((N_GROUPS, tm, LANES), lambda i: (0, i, 0)),
                  _const_spec((D_MODEL, D_MODEL)),
                  _const_spec((1, D_MODEL)),
                  _const_spec((D_MODEL, 2 * d_ff)),
                  _const_spec((d_ff, D_MODEL)),
                  _const_spec((1, D_MODEL))],
        out_specs=pl.BlockSpec((tm, D_MODEL), lambda i: (i, 0)),
        out_shape=jax.ShapeDtypeStruct((n, D_MODEL), F32),
        compiler_params=_params(),
    )(x2d, o_hm, w_o, norm_w, w_in, w_out, norm_final)


def _rope_tables(pos):
    half = HEAD_DIM // 2
    inv_freq = ROPE_THETA ** (-jnp.arange(half, dtype=F32) / half)
    ang = pos.astype(F32)[:, None] * inv_freq[None, :]
    cos = jnp.cos(ang)
    sin = jnp.sin(ang)
    cos_tok = jnp.tile(cos, (1, LANES // half))
    sin_tok = jnp.tile(jnp.concatenate([-sin, sin], axis=1), (1, LANES // HEAD_DIM))
    return cos_tok, sin_tok, cos.T, sin.T


def _lambda_init(layer):
    return 0.8 - 0.6 * math.exp(-0.3 * layer)


def _pair_columns(c_t):
    b, _, t = c_t.shape
    return jnp.swapaxes(c_t.reshape(b, N_GROUPS, 2, t), 2, 3)


def _feature_major_to_tokens(a, heads):
    *lead, b, _, t = a.shape
    a = a.reshape(*lead, b, heads, D_MODEL // heads, t)
    return jnp.moveaxis(a, -1, -3)


def kernel(x_prompt, x_sample, cache_diff_k, cache_diff_v, cache_fox_k, cache_fox_v, cache_fox_logf,
           norm_mix, norm_ffn, norm_final, diff_w_qkv, diff_w_o, diff_lambda, diff_subln,
           fox_w_qkvf, fox_b_f, fox_w_o, ffn_w_in, ffn_w_out):
    bp, s_len, _ = x_prompt.shape
    bs, t_len, _ = x_sample.shape
    p_len = cache_diff_k.shape[2]
    depth = norm_mix.shape[0]
    n_diff = diff_w_qkv.shape[0]
    n_fox = fox_w_qkvf.shape[0]
    dv_heads = cache_diff_v.shape[3]

    ns = bs * t_len
    tm_s = min(ROW_BLOCK, ns)
    xp3 = x_prompt
    xs3 = x_sample.reshape(ns // tm_s, tm_s, D_MODEL)
    xp = x_prompt.reshape(bp * s_len, D_MODEL)
    xs = x_sample.reshape(ns, D_MODEL)

    tab_p = _rope_tables(jnp.arange(s_len, dtype=jnp.int32))
    tab_s = _rope_tables(p_len + jnp.arange(t_len, dtype=jnp.int32))
    reps = tm_s // t_len
    tab_s = (jnp.tile(tab_s[0], (reps, 1)), jnp.tile(tab_s[1], (reps, 1)),
             jnp.tile(tab_s[2], (1, reps)), jnp.tile(tab_s[3], (1, reps)))

    q_scale = HEAD_DIM ** -0.5 * LOG2_E
    d = D_MODEL
    wd = [((diff_w_qkv[j, :, :d] * q_scale).astype(BF16),
           diff_w_qkv[j, :, d:2 * d].T.astype(BF16),
           diff_w_qkv[j, :, 2 * d:].astype(BF16)) for j in range(n_diff)]
    pad = LANES - N_SCORE_HEADS
    wf = [((fox_w_qkvf[j, :, :d] * q_scale).astype(BF16),
           fox_w_qkvf[j, :, d:2 * d].T.astype(BF16),
           fox_w_qkvf[j, :, 2 * d:3 * d].T.astype(BF16),
           jnp.concatenate([fox_w_qkvf[j, :, 3 * d:].T, jnp.zeros((pad, d), F32)], axis=0).astype(BF16))
          for j in range(n_fox)]
    b_f = [jnp.concatenate([fox_b_f[j], jnp.zeros((pad,), F32)]).reshape(LANES, 1) for j in range(n_fox)]
    w_o_d = diff_w_o.astype(BF16)
    w_o_f = fox_w_o.astype(BF16)
    w_in = ffn_w_in.astype(BF16)
    w_out = ffn_w_out.astype(BF16)
    norm_mix = norm_mix.reshape(depth, 1, D_MODEL)
    norm_ffn = norm_ffn.reshape(depth, 1, D_MODEL)
    norm_final = norm_final.reshape(1, D_MODEL)

    def feature_major(c):
        l, b, p, hh, dd = c.shape
        return jnp.transpose(c, (0, 1, 3, 4, 2)).reshape(l, b, hh * dd, p)

    ckd = feature_major(cache_diff_k)
    cvd = cache_diff_v.reshape(n_diff, bs, p_len * dv_heads, LANES)
    ckf = feature_major(cache_fox_k)
    cvf = feature_major(cache_fox_v)

    def sample_feature_major(a):
        a = a.reshape(-1, D_MODEL, tm_s // t_len, t_len)
        return jnp.transpose(a, (0, 2, 1, 3)).reshape(bs, D_MODEL, t_len)

    prev_d = prev_f = None
    samp = {k: [] for k in ("dk", "dv", "fk", "fv", "fl")}
    for i in range(depth):
        j = i // 2
        final = i == depth - 1
        if i % 2 == 0:
            lam_init = _lambda_init(i)
            sub = diff_subln[j].reshape(1, LANES)
            qh, kth, vh, kt32, v32 = _project(xp3, norm_mix[i], wd[j], rotary=True, tables=tab_p,
                                              layer=j, n_layers=n_diff, prev=prev_d)
            prev_d = (kt32, v32)
            op = _prompt_attention(qh, kth, vh, bp, diff=True, lam_init=lam_init,
                                   lam_params=diff_lambda[j], subln_w=sub)
            qh, kth, vh, kt32, v32 = _project(xs3, norm_mix[i], wd[j], rotary=True, tables=tab_s)
            os_ = _sample_attention(qh, sample_feature_major(kth), vh, ckd, cvd, j, diff=True,
                                    lam_init=lam_init, lam_params=diff_lambda[j], subln_w=sub)
            samp["dk"].append(_feature_major_to_tokens(sample_feature_major(kt32[0]), N_SCORE_HEADS))
            samp["dv"].append(v32[0].reshape(bs, t_len, dv_heads, LANES))
            w_o = w_o_d[j]
        else:
            qh, kth, vth, kt32, vt32, lft = _project(xp3, norm_mix[i], wf[j], rotary=False, b_f=b_f[j],
                                                    layer=j, n_layers=n_fox, prev=prev_f)
            prev_f = (kt32, vt32, lft)
            c_t = _cumsum_time(lft[j], min(512, s_len))
            ck = c_t.reshape(bp, N_SCORE_HEADS, kth.shape[1], kth.shape[3])
            op = _prompt_attention(qh, kth, vth, bp, diff=False, cq=_pair_columns(c_t), ck=ck)
            qh, kth, vth, kt32, vt32, lft_s = _project(xs3, norm_mix[i], wf[j], rotary=False, b_f=b_f[j])
            lf_s = jnp.transpose(lft_s[0].reshape(ns // tm_s, N_SCORE_HEADS, tm_s // t_len, t_len),
                                 (0, 2, 1, 3)).reshape(bs, N_SCORE_HEADS, t_len)
            cc = _cumsum_time(jnp.swapaxes(cache_fox_logf[j], 1, 2), min(512, p_len))
            cn = _cumsum_time(lf_s, t_len)
            os_ = _sample_attention(qh, sample_feature_major(kth), sample_feature_major(vth), ckf, cvf, j,
                                    diff=False, cq=_pair_columns(cn), cn=cn, cc=cc)
            samp["fk"].append(_feature_major_to_tokens(sample_feature_major(kt32[0]), N_SCORE_HEADS))
            samp["fv"].append(_feature_major_to_tokens(sample_feature_major(vt32[0]), N_SCORE_HEADS))
            samp["fl"].append(jnp.swapaxes(lf_s, 1, 2))
            w_o = w_o_f[j]
        xp = _mlp(xp, op, w_o, norm_ffn[i], w_in[i], w_out[i], norm_final, final=final)
        xs = _mlp(xs, os_, w_o, norm_ffn[i], w_in[i], w_out[i], norm_final, final=final)
        xp3 = xp.reshape(bp, s_len, D_MODEL)
        xs3 = xs.reshape(ns // tm_s, tm_s, D_MODEL)

    dk32, dv32 = prev_d
    fk32, fv32, fl32 = prev_f
    return (xp.reshape(bp, s_len, D_MODEL), xs.reshape(bs, t_len, D_MODEL),
            _feature_major_to_tokens(dk32, N_SCORE_HEADS),
            dv32.reshape(n_diff, bp, s_len, dv_heads, LANES),
            _feature_major_to_tokens(fk32, N_SCORE_HEADS),
            _feature_major_to_tokens(fv32, N_SCORE_HEADS),
            jnp.swapaxes(fl32, 2, 3),
            jnp.stack(samp["dk"]), jnp.stack(samp["dv"]), jnp.stack(samp["fk"]),
            jnp.stack(samp["fv"]), jnp.stack(samp["fl"]))
```

```python
import functools
import math

import jax
import jax.numpy as jnp
from jax import lax
from jax.experimental import pallas as pl
from jax.experimental.pallas import tpu as pltpu

D_MODEL = 1024
CHUNK = 64
HEAD_DIM = 64
N_SCORE_HEADS = 16
ROPE_THETA = 10000.0
NORM_EPS = 1e-6
SUBLN_EPS = 1e-5

LANES = 128
N_GROUPS = D_MODEL // LANES
VMEM_LIMIT_BYTES = 56 * 1024 * 1024
ROW_BLOCK = 512
KEY_BLOCKS_PER_STEP = 2

F32 = jnp.float32
BF16 = jnp.bfloat16
NEG_INF = float("-inf")
LOG2_E = math.log2(math.e)
NT_DIMS = (((1,), (1,)), ((), ()))


def _params():
    return pltpu.CompilerParams(vmem_limit_bytes=VMEM_LIMIT_BYTES)


def _const_spec(shape):
    nd = len(shape)
    return pl.BlockSpec(shape, lambda *_: (0,) * nd, pipeline_mode=pl.Buffered(1))


def _rmsnorm_rows(x, w, eps):
    ms = jnp.mean(x * x, axis=-1, keepdims=True)
    return x * lax.rsqrt(ms + eps) * w


def _proj_body(*refs, rotary, n_prev, slot):
    if rotary:
        (x_ref, nw_ref, wq_ref, wkt_ref, wv_ref, cos_ref, sin_ref, cost_ref, sint_ref) = refs[:9]
        qh_ref, kth_ref, vh_ref, kt_ref, v_ref = refs[9 + n_prev:]
    else:
        (x_ref, nw_ref, wq_ref, wkt_ref, wvt_ref, wft_ref, bf_ref) = refs[:7]
        qh_ref, kth_ref, vth_ref, kt_ref, vt_ref, lft_ref = refs[7 + n_prev:]
    tm = x_ref.shape[1]
    for ref in refs[(9 if rotary else 7) + n_prev + 3:]:
        for other in range(ref.shape[0]):
            if other != slot:
                ref[other] = jnp.zeros(ref.shape[1:], F32)
    h = _rmsnorm_rows(x_ref[0], nw_ref[...], NORM_EPS).astype(BF16)
    q = jnp.dot(h, wq_ref[...], preferred_element_type=F32)
    kt = lax.dot_general(wkt_ref[...], h, NT_DIMS, preferred_element_type=F32)
    half = HEAD_DIM // 2
    if rotary:
        cs = cos_ref[...]
        sn = sin_ref[...]
        lane = lax.broadcasted_iota(jnp.int32, cs.shape, 1)
        first_half = (lane & half) == 0
        for g in range(N_GROUPS):
            qg = q[:, LANES * g:LANES * (g + 1)]
            qp = jnp.where(first_half, pltpu.roll(qg, LANES - half, 1), pltpu.roll(qg, half, 1))
            qh_ref[g] = (qg * cs + qp * sn).astype(BF16)
        ct = cost_ref[...]
        st = sint_ref[...]
        for hh in range(N_SCORE_HEADS):
            r0 = HEAD_DIM * hh
            a = kt[r0:r0 + half]
            b = kt[r0 + half:r0 + HEAD_DIM]
            ka = a * ct - b * st
            kb = b * ct + a * st
            kt_ref[slot, 0, r0:r0 + half, :] = ka
            kt_ref[slot, 0, r0 + half:r0 + HEAD_DIM, :] = kb
            kth_ref[0, 0, r0:r0 + half, :] = ka.astype(BF16)
            kth_ref[0, 0, r0 + half:r0 + HEAD_DIM, :] = kb.astype(BF16)
        v = jnp.dot(h, wv_ref[...], preferred_element_type=F32)
        for g in range(N_GROUPS):
            vg = v[:, LANES * g:LANES * (g + 1)]
            vh_ref[g] = vg.astype(BF16)
            v_ref[slot, pl.ds(g, tm, stride=N_GROUPS), :] = vg
    else:
        for g in range(N_GROUPS):
            qh_ref[g] = q[:, LANES * g:LANES * (g + 1)].astype(BF16)
        kt_ref[slot, 0] = kt
        kth_ref[0, 0] = kt.astype(BF16)
        vt = lax.dot_general(wvt_ref[...], h, NT_DIMS, preferred_element_type=F32)
        vt_ref[slot, 0] = vt
        vth_ref[0, 0] = vt.astype(BF16)
        z = lax.dot_general(wft_ref[...], h, NT_DIMS, preferred_element_type=F32) + bf_ref[...]
        lf = jnp.minimum(z, 0.0) - jnp.log1p(jnp.exp(-jnp.abs(z)))
        lft_ref[slot, 0] = lf[:N_SCORE_HEADS]


def _project(x3, norm_w, weights, *, rotary, tables=None, b_f=None, layer=0, n_layers=1, prev=None):
    bx, tx, _ = x3.shape
    tm = min(ROW_BLOCK, tx)
    n = bx * tx
    t_blocks = tx // tm
    grid = (t_blocks, bx)
    prev = list(prev) if prev is not None else []
    n_in_fixed = 9 if rotary else 7
    xspec = pl.BlockSpec((1, tm, D_MODEL), lambda t, b: (b, t, 0))
    wspecs = [_const_spec(w.shape) for w in weights]
    hm = jax.ShapeDtypeStruct((N_GROUPS, n, LANES), BF16)
    hm_spec = pl.BlockSpec((N_GROUPS, tm, LANES), lambda t, b: (0, b * t_blocks + t, 0))
    fm16 = jax.ShapeDtypeStruct((bx, t_blocks, D_MODEL, tm), BF16)
    fm16_spec = pl.BlockSpec((1, 1, D_MODEL, tm), lambda t, b: (b, t, 0, 0))
    fm32 = jax.ShapeDtypeStruct((n_layers, bx, D_MODEL, tx), F32)
    n_slots, lead, slot = (1, layer, 0) if prev else (n_layers, 0, layer)
    fm32_spec = pl.BlockSpec((n_slots, 1, D_MODEL, tm), lambda t, b: (lead, b, 0, t))
    if rotary:
        cos, sin, cos_t, sin_t = tables
        extra_in = [cos, sin, cos_t, sin_t]
        extra_specs = [pl.BlockSpec((tm, LANES), lambda t, b: (t, 0)),
                       pl.BlockSpec((tm, LANES), lambda t, b: (t, 0)),
                       pl.BlockSpec((HEAD_DIM // 2, tm), lambda t, b: (0, t)),
                       pl.BlockSpec((HEAD_DIM // 2, tm), lambda t, b: (0, t))]
        out_shape = [hm, fm16, hm, fm32,
                     jax.ShapeDtypeStruct((n_layers, n * N_GROUPS, LANES), F32)]
        out_specs = [hm_spec, fm16_spec, hm_spec, fm32_spec,
                     pl.BlockSpec((n_slots, tm * N_GROUPS, LANES), lambda t, b: (lead, b * t_blocks + t, 0))]
        n_f32 = 2
    else:
        extra_in = [b_f]
        extra_specs = [_const_spec(b_f.shape)]
        out_shape = [hm, fm16, fm16, fm32, fm32,
                     jax.ShapeDtypeStruct((n_layers, bx, N_SCORE_HEADS, tx), F32)]
        out_specs = [hm_spec, fm16_spec, fm16_spec, fm32_spec, fm32_spec,
                     pl.BlockSpec((n_slots, 1, N_SCORE_HEADS, tm), lambda t, b: (lead, b, 0, t))]
        n_f32 = 3
    assert len(prev) in (0, n_f32)
    aliases = {n_in_fixed + k: 3 + k for k in range(len(prev))}
    return pl.pallas_call(
        functools.partial(_proj_body, rotary=rotary, n_prev=len(prev), slot=slot),
        grid=grid,
        in_specs=[xspec, _const_spec((1, D_MODEL))] + wspecs + extra_specs
                 + [pl.BlockSpec(memory_space=pl.ANY)] * len(prev),
        out_specs=out_specs,
        out_shape=out_shape,
        input_output_aliases=aliases,
        compiler_params=_params(),
    )(x3, norm_w, *weights, *extra_in, *prev)


def _cumsum_body(x_ref, o_ref, *, chunk):
    t = x_ref.shape[2]
    r = lax.broadcasted_iota(jnp.int32, (chunk, chunk), 0)
    c = lax.broadcasted_iota(jnp.int32, (chunk, chunk), 1)
    upper = (r <= c).astype(BF16)
    carry = jnp.zeros((x_ref.shape[1], 1), F32)
    for i in range(t // chunk):
        x = x_ref[0, :, chunk * i:chunk * (i + 1)]
        hi = x.astype(BF16)
        r1 = x - hi.astype(F32)
        mid = r1.astype(BF16)
        lo = (r1 - mid.astype(F32)).astype(BF16)
        cs = (jnp.dot(hi, upper, preferred_element_type=F32)
              + jnp.dot(mid, upper, preferred_element_type=F32)
              + jnp.dot(lo, upper, preferred_element_type=F32)) + carry
        o_ref[0, :, chunk * i:chunk * (i + 1)] = cs * LOG2_E
        carry = cs[:, chunk - 1:chunk]


def _cumsum_time(lf_t, chunk):
    b, h, t = lf_t.shape
    return pl.pallas_call(
        functools.partial(_cumsum_body, chunk=chunk),
        grid=(b,),
        in_specs=[pl.BlockSpec((1, h, t), lambda i: (i, 0, 0))],
        out_specs=pl.BlockSpec((1, h, t), lambda i: (i, 0, 0)),
        out_shape=jax.ShapeDtypeStruct((b, h, t), F32),
        compiler_params=_params(),
    )(lf_t)


def _diff_lambda(lam_ref, lam_init):
    lp = lam_ref[...]
    a = jnp.sum(lp[0:1] * lp[1:2], axis=-1, keepdims=True)
    b = jnp.sum(lp[2:3] * lp[3:4], axis=-1, keepdims=True)
    return jnp.exp(a) - jnp.exp(b) + lam_init


def _diff_combine(o1, o2, lam, subln_w, lam_init):
    o = o1 - lam * o2
    return _rmsnorm_rows(o, subln_w, SUBLN_EPS) * (1.0 - lam_init)


def _stack_heads(qb):
    low = lax.broadcasted_iota(jnp.int32, qb.shape, 1) < HEAD_DIM
    zero = jnp.zeros_like(qb)
    return jnp.concatenate([jnp.where(low, qb, zero), jnp.where(low, zero, qb)], axis=0)


def _head_rows(x, t, r0):
    if r0 == 0:
        return x
    return jnp.concatenate([x[r0:t], x[t + r0:]], axis=0)


def _update_head_rows(x, t, r0, y, op):
    if r0 == 0:
        return op(x, y)
    h = t - r0
    return jnp.concatenate([x[:r0], op(x[r0:t], y[:h]), x[t:t + r0], op(x[t + r0:], y[h:])], axis=0)


def _fold_lanes(x, op):
    acc = x[:, :LANES]
    for c in range(1, x.shape[1] // LANES):
        acc = op(acc, x[:, LANES * c:LANES * (c + 1)])
    return acc


def _merge_heads(o2, t, diff, lam, subln_w, lam_init):
    o_a, o_b = o2[:t], o2[t:]
    if diff:
        out = _diff_combine(o_a, o_b, lam, subln_w, lam_init)
    else:
        low = lax.broadcasted_iota(jnp.int32, o_a.shape, 1) < HEAD_DIM
        out = jnp.where(low, o_a, o_b)
    return out.astype(BF16)


def _prompt_attn_body(*refs, diff, lam_init, blk):
    if diff:
        q_ref, kt_ref, v_ref, lam_ref, sub_ref, o_ref, s_ref, va_ref = refs
        va_ref[:, :LANES] = v_ref[0]
        va_ref[:, LANES:] = jnp.ones((va_ref.shape[0], LANES), BF16)
    else:
        q_ref, kt_ref, vt_ref, cq_ref, ck_ref, o_ref, s_ref, va_ref = refs
        va_ref[:, :LANES, :] = vt_ref[0]
        va_ref[:, LANES:, :] = jnp.ones((va_ref.shape[0], LANES, va_ref.shape[2]), BF16)
    t = q_ref.shape[1]
    g = pl.program_id(0)
    bq = KEY_BLOCKS_PER_STEP * blk
    if diff:
        lam = _diff_lambda(lam_ref, lam_init)
        subln_w = sub_ref[...]
    else:
        lam = subln_w = None
    neg_inf = jnp.full((2 * bq, LANES), NEG_INF, F32)

    def q_step(i, _):
        q0 = pl.multiple_of(i * bq, bq)
        q2 = _stack_heads(q_ref[0, pl.ds(q0, bq), :])
        if not diff:
            cq = cq_ref[0, 0, pl.ds(q0, bq), :]
            cq2 = jnp.concatenate([cq[:, 0:1], cq[:, 1:2]], axis=0)

        def score(c, r0=0):
            q_sub = _head_rows(q2, bq, r0)
            s = jnp.dot(q_sub, kt_ref[0, c], preferred_element_type=F32)
            if not diff:
                h = bq - r0
                ck_a = ck_ref[0, 2 * g, pl.ds(c, 1), :]
                ck_b = ck_ref[0, 2 * g + 1, pl.ds(c, 1), :]
                s = jnp.concatenate([s[:h] + (cq2[r0:bq] - ck_a), s[h:] + (cq2[bq + r0:] - ck_b)], axis=0)
            return s

        def weigh(c, mrow_sub, r0=0):
            h = bq - r0
            if r0 == 0:
                sc = s_ref[c]
            else:
                sc = jnp.concatenate([s_ref[c, r0:bq, :], s_ref[c, bq + r0:, :]], axis=0)
            p = jnp.exp2(sc - mrow_sub).astype(BF16)
            if diff:
                k0 = pl.multiple_of(c * blk, blk)
                return jnp.dot(p, va_ref[pl.ds(k0, blk), :], preferred_element_type=F32)
            return lax.dot_general(p, va_ref[c], NT_DIMS, preferred_element_type=F32)

        def pass1(j, m):
            for u in range(KEY_BLOCKS_PER_STEP):
                c = KEY_BLOCKS_PER_STEP * j + u
                s = score(c)
                s_ref[c] = s
                m = jnp.maximum(m, _fold_lanes(s, jnp.maximum))
            return m

        m = lax.fori_loop(0, i, pass1, neg_inf)
        for u in range(KEY_BLOCKS_PER_STEP):
            c = KEY_BLOCKS_PER_STEP * i + u
            r0 = u * blk
            h = bq - r0
            qpos = r0 + lax.broadcasted_iota(jnp.int32, (2 * h, blk), 0) % h
            kpos = r0 + lax.broadcasted_iota(jnp.int32, (2 * h, blk), 1)
            allowed = (kpos // CHUNK) <= (qpos // CHUNK) if diff else kpos <= qpos
            s = jnp.where(allowed, score(c, r0), NEG_INF)
            s_ref[c, r0:bq, :] = s[:h]
            s_ref[c, bq + r0:, :] = s[h:]
            m = _update_head_rows(m, bq, r0, _fold_lanes(s, jnp.maximum), jnp.maximum)
        mrow = jnp.max(m, axis=-1, keepdims=True)

        def pass2(j, acc):
            for u in range(KEY_BLOCKS_PER_STEP):
                acc = acc + weigh(KEY_BLOCKS_PER_STEP * j + u, mrow)
            return acc

        acc = lax.fori_loop(0, i, pass2, jnp.zeros((2 * bq, 2 * LANES), F32))
        for u in range(KEY_BLOCKS_PER_STEP):
            r0 = u * blk
            pv = weigh(KEY_BLOCKS_PER_STEP * i + u, _head_rows(mrow, bq, r0), r0)
            acc = _update_head_rows(acc, bq, r0, pv, jnp.add)
        o2 = acc[:, :LANES] / acc[:, LANES:]
        o_ref[0, pl.ds(q0, bq), :] = _merge_heads(o2, bq, diff, lam, subln_w, lam_init)
        return 0

    lax.fori_loop(0, t // bq, q_step, 0)


def _prompt_attention(qh, kth, v_op, batch, *, diff, lam_init=0.0, lam_params=None, subln_w=None,
                      cq=None, ck=None):
    n = qh.shape[1]
    t = n // batch
    blk = kth.shape[3]
    n_blk = t // blk
    tok = pl.BlockSpec((1, t, LANES), lambda g, b: (g, b, 0))
    feat = pl.BlockSpec((1, n_blk, LANES, blk), lambda g, b: (b, 0, g, 0))
    if diff:
        extra_in = [lam_params, subln_w]
        extra_specs = [_const_spec(lam_params.shape), _const_spec(subln_w.shape)]
        v_spec = tok
    else:
        extra_in = [cq, ck]
        extra_specs = [pl.BlockSpec((1, 1, t, 2), lambda g, b: (b, g, 0, 0)),
                       pl.BlockSpec((1, N_SCORE_HEADS, n_blk, blk), lambda g, b: (b, 0, 0, 0))]
        v_spec = feat
    return pl.pallas_call(
        functools.partial(_prompt_attn_body, diff=diff, lam_init=lam_init, blk=blk),
        grid=(N_GROUPS, batch),
        in_specs=[tok, feat, v_spec] + extra_specs,
        out_specs=tok,
        out_shape=jax.ShapeDtypeStruct((N_GROUPS, n, LANES), BF16),
        scratch_shapes=[pltpu.VMEM((n_blk, 2 * KEY_BLOCKS_PER_STEP * blk, blk), F32),
                        pltpu.VMEM((t, 2 * LANES) if diff else (n_blk, 2 * LANES, blk), BF16)],
        compiler_params=_params(),
    )(qh, kth, v_op, *extra_in)


def _sample_attn_body(*refs, diff, lam_init):
    if diff:
        q_ref, ktn_ref, vn_ref, ktc_ref, vc_ref, lam_ref, sub_ref, o_ref = refs
        lam = _diff_lambda(lam_ref, lam_init)
        subln_w = sub_ref[...]
    else:
        q_ref, ktn_ref, vtn_ref, ktc_ref, vtc_ref, cq_ref, cn_ref, cc_ref, o_ref = refs
        lam = subln_w = None
    ts = q_ref.shape[1]
    p_len = ktc_ref.shape[3]
    if not diff:
        row = lax.broadcasted_iota(jnp.int32, (2 * ts, ts), 0) % ts
        col = lax.broadcasted_iota(jnp.int32, (2 * ts, ts), 1)
        causal = col <= row

    def group(g, _):
        r0 = pl.multiple_of(g * LANES, LANES)
        q2 = _stack_heads(q_ref[g])
        ktc = ktc_ref[0, 0, pl.ds(r0, LANES), :].astype(BF16)
        sc = jnp.dot(q2, ktc, preferred_element_type=F32)
        sn = jnp.dot(q2, ktn_ref[0, pl.ds(r0, LANES), :], preferred_element_type=F32)
        if not diff:
            cq = cq_ref[0, g]
            cq2 = jnp.concatenate([cq[:, 0:1], cq[:, 1:2]], axis=0)
            cc_a = cc_ref[0, pl.ds(2 * g, 1), :]
            cc_b = cc_ref[0, pl.ds(2 * g + 1, 1), :]
            cn_a = cn_ref[0, pl.ds(2 * g, 1), :]
            cn_b = cn_ref[0, pl.ds(2 * g + 1, 1), :]
            sc = jnp.concatenate(
                [sc[:ts] + ((cq2[:ts] + cc_a[:, p_len - 1:p_len]) - cc_a),
                 sc[ts:] + ((cq2[ts:] + cc_b[:, p_len - 1:p_len]) - cc_b)], axis=0)
            sn = jnp.concatenate([sn[:ts] + (cq2[:ts] - cn_a), sn[ts:] + (cq2[ts:] - cn_b)], axis=0)
            sn = jnp.where(causal, sn, NEG_INF)
        m = jnp.maximum(jnp.max(sc, axis=-1, keepdims=True), jnp.max(sn, axis=-1, keepdims=True))
        pc = jnp.exp2(sc - m)
        pn = jnp.exp2(sn - m)
        l = jnp.sum(pc, axis=-1, keepdims=True) + jnp.sum(pn, axis=-1, keepdims=True)
        if diff:
            vc = vc_ref[0, 0, pl.ds(g, p_len, stride=N_GROUPS), :].astype(BF16)
            acc = (jnp.dot(pc.astype(BF16), vc, preferred_element_type=F32)
                   + jnp.dot(pn.astype(BF16), vn_ref[g], preferred_element_type=F32))
        else:
            vtc = vtc_ref[0, 0, pl.ds(r0, LANES), :].astype(BF16)
            acc = (lax.dot_general(pc.astype(BF16), vtc, NT_DIMS, preferred_element_type=F32)
                   + lax.dot_general(pn.astype(BF16), vtn_ref[0, pl.ds(r0, LANES), :], NT_DIMS,
                                     preferred_element_type=F32))
        o_ref[g] = _merge_heads(acc / l, ts, diff, lam, subln_w, lam_init)
        return 0

    lax.fori_loop(0, N_GROUPS, group, 0)


def _sample_attention(qh, ktn, v_new, cache_kt, cache_v, layer, *, diff, lam_init=0.0, lam_params=None,
                      subln_w=None, cq=None, cn=None, cc=None):
    _, batch, _, p_len = cache_kt.shape
    n = qh.shape[1]
    ts = n // batch
    tok = pl.BlockSpec((N_GROUPS, ts, LANES), lambda b: (0, b, 0))
    feat = pl.BlockSpec((1, D_MODEL, ts), lambda b: (b, 0, 0))
    cache_feat = pl.BlockSpec((1, 1, D_MODEL, p_len), lambda b: (layer, b, 0, 0))
    if diff:
        extra_in = [lam_params, subln_w]
        extra_specs = [_const_spec(lam_params.shape), _const_spec(subln_w.shape)]
        v_specs = [tok, cache_feat, pl.BlockSpec((1, 1, p_len * N_GROUPS, LANES), lambda b: (layer, b, 0, 0))]
    else:
        extra_in = [cq, cn, cc]
        extra_specs = [pl.BlockSpec((1, N_GROUPS, ts, 2), lambda b: (b, 0, 0, 0)),
                       pl.BlockSpec((1, N_SCORE_HEADS, ts), lambda b: (b, 0, 0)),
                       pl.BlockSpec((1, N_SCORE_HEADS, p_len), lambda b: (b, 0, 0))]
        v_specs = [feat, cache_feat, cache_feat]
    return pl.pallas_call(
        functools.partial(_sample_attn_body, diff=diff, lam_init=lam_init),
        grid=(batch,),
        in_specs=[tok, feat] + v_specs + extra_specs,
        out_specs=tok,
        out_shape=jax.ShapeDtypeStruct((N_GROUPS, n, LANES), BF16),
        compiler_params=_params(),
    )(qh, ktn, v_new, cache_kt, cache_v, *extra_in)


def _mlp_body(x_ref, o_ref, wo_ref, nw_ref, win_ref, wout_ref, nf_ref, y_ref, *, final):
    d_ff = wout_ref.shape[0]
    o = jnp.concatenate([o_ref[g] for g in range(N_GROUPS)], axis=1)
    x1 = x_ref[...] + jnp.dot(o, wo_ref[...], preferred_element_type=F32)
    h = _rmsnorm_rows(x1, nw_ref[...], NORM_EPS).astype(BF16)
    gate = jnp.dot(h, win_ref[:, :d_ff], preferred_element_type=F32)
    up = jnp.dot(h, win_ref[:, d_ff:], preferred_element_type=F32)
    act = (gate * jax.nn.sigmoid(gate) * up).astype(BF16)
    x2 = x1 + jnp.dot(act, wout_ref[...], preferred_element_type=F32)
    if final:
        x2 = _rmsnorm_rows(x2, nf_ref[...], NORM_EPS)
    y_ref[...] = x2


def _mlp(x2d, o_hm, w_o, norm_w, w_in, w_out, norm_final, *, final):
    n = x2d.shape[0]
    tm = min(ROW_BLOCK, n)
    d_ff = w_out.shape[0]
    return pl.pallas_call(
        functools.partial(_mlp_body, final=final),
        grid=(n // tm,),
        in_specs=[pl.BlockSpec((tm, D_MODEL), lambda i: (i, 0)),
                  pl.BlockSpec((N_GROUPS, tm, LANES), lambda i: (0, i, 0)),
                  _const_spec((D_MODEL, D_MODEL)),
                  _const_spec((1, D_MODEL)),
                  _const_spec((D_MODEL, 2 * d_ff)),
                  _const_spec((d_ff, D_MODEL)),
                  _const_spec((1, D_MODEL))],
        out_specs=pl.BlockSpec((tm, D_MODEL), lambda i: (i, 0)),
        out_shape=jax.ShapeDtypeStruct((n, D_MODEL), F32),
        compiler_params=_params(),
    )(x2d, o_hm, w_o, norm_w, w_in, w_out, norm_final)


def _rope_tables(pos):
    half = HEAD_DIM // 2
    inv_freq = ROPE_THETA ** (-jnp.arange(half, dtype=F32) / half)
    ang = pos.astype(F32)[:, None] * inv_freq[None, :]
    cos = jnp.cos(ang)
    sin = jnp.sin(ang)
    cos_tok = jnp.tile(cos, (1, LANES // half))
    sin_tok = jnp.tile(jnp.concatenate([-sin, sin], axis=1), (1, LANES // HEAD_DIM))
    return cos_tok, sin_tok, cos.T, sin.T


def _lambda_init(layer):
    return 0.8 - 0.6 * math.exp(-0.3 * layer)


def _pair_columns(c_t):
    b, _, t = c_t.shape
    return jnp.swapaxes(c_t.reshape(b, N_GROUPS, 2, t), 2, 3)


def _feature_major_to_tokens(a, heads):
    *lead, b, _, t = a.shape
    a = a.reshape(*lead, b, heads, D_MODEL // heads, t)
    return jnp.moveaxis(a, -1, -3)


def kernel(x_prompt, x_sample, cache_diff_k, cache_diff_v, cache_fox_k, cache_fox_v, cache_fox_logf,
           norm_mix, norm_ffn, norm_final, diff_w_qkv, diff_w_o, diff_lambda, diff_subln,
           fox_w_qkvf, fox_b_f, fox_w_o, ffn_w_in, ffn_w_out):
    bp, s_len, _ = x_prompt.shape
    bs, t_len, _ = x_sample.shape
    p_len = cache_diff_k.shape[2]
    depth = norm_mix.shape[0]
    n_diff = diff_w_qkv.shape[0]
    n_fox = fox_w_qkvf.shape[0]
    dv_heads = cache_diff_v.shape[3]

    ns = bs * t_len
    tm_s = min(ROW_BLOCK, ns)
    xp3 = x_prompt
    xs3 = x_sample.reshape(ns // tm_s, tm_s, D_MODEL)
    xp = x_prompt.reshape(bp * s_len, D_MODEL)
    xs = x_sample.reshape(ns, D_MODEL)

    tab_p = _rope_tables(jnp.arange(s_len, dtype=jnp.int32))
    tab_s = _rope_tables(p_len + jnp.arange(t_len, dtype=jnp.int32))
    reps = tm_s // t_len
    tab_s = (jnp.tile(tab_s[0], (reps, 1)), jnp.tile(tab_s[1], (reps, 1)),
             jnp.tile(tab_s[2], (1, reps)), jnp.tile(tab_s[3], (1, reps)))

    q_scale = HEAD_DIM ** -0.5 * LOG2_E
    d = D_MODEL
    wd = [((diff_w_qkv[j, :, :d] * q_scale).astype(BF16),
           diff_w_qkv[j, :, d:2 * d].T.astype(BF16),
           diff_w_qkv[j, :, 2 * d:].astype(BF16)) for j in range(n_diff)]
    pad = LANES - N_SCORE_HEADS
    wf = [((fox_w_qkvf[j, :, :d] * q_scale).astype(BF16),
           fox_w_qkvf[j, :, d:2 * d].T.astype(BF16),
           fox_w_qkvf[j, :, 2 * d:3 * d].T.astype(BF16),
           jnp.concatenate([fox_w_qkvf[j, :, 3 * d:].T, jnp.zeros((pad, d), F32)], axis=0).astype(BF16))
          for j in range(n_fox)]
    b_f = [jnp.concatenate([fox_b_f[j], jnp.zeros((pad,), F32)]).reshape(LANES, 1) for j in range(n_fox)]
    w_o_d = diff_w_o.astype(BF16)
    w_o_f = fox_w_o.astype(BF16)
    w_in = ffn_w_in.astype(BF16)
    w_out = ffn_w_out.astype(BF16)
    norm_mix = norm_mix.reshape(depth, 1, D_MODEL)
    norm_ffn = norm_ffn.reshape(depth, 1, D_MODEL)
    norm_final = norm_final.reshape(1, D_MODEL)

    def feature_major(c):
        l, b, p, hh, dd = c.shape
        return jnp.transpose(c, (0, 1, 3, 4, 2)).reshape(l, b, hh * dd, p)

    ckd = feature_major(cache_diff_k)
    cvd = cache_diff_v.reshape(n_diff, bs, p_len * dv_heads, LANES)
    ckf = feature_major(cache_fox_k)
    cvf = feature_major(cache_fox_v)

    def sample_feature_major(a):
        a = a.reshape(-1, D_MODEL, tm_s // t_len, t_len)
        return jnp.transpose(a, (0, 2, 1, 3)).reshape(bs, D_MODEL, t_len)

    prev_d = prev_f = None
    samp = {k: [] for k in ("dk", "dv", "fk", "fv", "fl")}
    for i in range(depth):
        j = i // 2
        final = i == depth - 1
        if i % 2 == 0:
            lam_init = _lambda_init(i)
            sub = diff_subln[j].reshape(1, LANES)
            qh, kth, vh, kt32, v32 = _project(xp3, norm_mix[i], wd[j], rotary=True, tables=tab_p,
                                              layer=j, n_layers=n_diff, prev=prev_d)
            prev_d = (kt32, v32)
            op = _prompt_attention(qh, kth, vh, bp, diff=True, lam_init=lam_init,
                                   lam_params=diff_lambda[j], subln_w=sub)
            qh, kth, vh, kt32, v32 = _project(xs3, norm_mix[i], wd[j], rotary=True, tables=tab_s)
            os_ = _sample_attention(qh, sample_feature_major(kth), vh, ckd, cvd, j, diff=True,
                                    lam_init=lam_init, lam_params=diff_lambda[j], subln_w=sub)
            samp["dk"].append(_feature_major_to_tokens(sample_feature_major(kt32[0]), N_SCORE_HEADS))
            samp["dv"].append(v32[0].reshape(bs, t_len, dv_heads, LANES))
            w_o = w_o_d[j]
        else:
            qh, kth, vth, kt32, vt32, lft = _project(xp3, norm_mix[i], wf[j], rotary=False, b_f=b_f[j],
                                                    layer=j, n_layers=n_fox, prev=prev_f)
            prev_f = (kt32, vt32, lft)
            c_t = _cumsum_time(lft[j], min(512, s_len))
            ck = c_t.reshape(bp, N_SCORE_HEADS, kth.shape[1], kth.shape[3])
            op = _prompt_attention(qh, kth, vth, bp, diff=False, cq=_pair_columns(c_t), ck=ck)
            qh, kth, vth, kt32, vt32, lft_s = _project(xs3, norm_mix[i], wf[j], rotary=False, b_f=b_f[j])
            lf_s = jnp.transpose(lft_s[0].reshape(ns // tm_s, N_SCORE_HEADS, tm_s // t_len, t_len),
                                 (0, 2, 1, 3)).reshape(bs, N_SCORE_HEADS, t_len)
            cc = _cumsum_time(jnp.swapaxes(cache_fox_logf[j], 1, 2), min(512, p_len))
            cn = _cumsum_time(lf_s, t_len)
            os_ = _sample_attention(qh, sample_feature_major(kth), sample_feature_major(vth), ckf, cvf, j,
                                    diff=False, cq=_pair_columns(cn), cn=cn, cc=cc)
            samp["fk"].append(_feature_major_to_tokens(sample_feature_major(kt32[0]), N_SCORE_HEADS))
            samp["fv"].append(_feature_major_to_tokens(sample_feature_major(vt32[0]), N_SCORE_HEADS))
            samp["fl"].append(jnp.swapaxes(lf_s, 1, 2))
            w_o = w_o_f[j]
        xp = _mlp(xp, op, w_o, norm_ffn[i], w_in[i], w_out[i], norm_final, final=final)
        xs = _mlp(xs, os_, w_o, norm_ffn[i], w_in[i], w_out[i], norm_final, final=final)
        xp3 = xp.reshape(bp, s_len, D_MODEL)
        xs3 = xs.reshape(ns // tm_s, tm_s, D_MODEL)

    dk32, dv32 = prev_d
    fk32, fv32, fl32 = prev_f
    return (xp.reshape(bp, s_len, D_MODEL), xs.reshape(bs, t_len, D_MODEL),
            _feature_major_to_tokens(dk32, N_SCORE_HEADS),
            dv32.reshape(n_diff, bp, s_len, dv_heads, LANES),
            _feature_major_to_tokens(fk32, N_SCORE_HEADS),
            _feature_major_to_tokens(fv32, N_SCORE_HEADS),
            jnp.swapaxes(fl32, 2, 3),
            jnp.stack(samp["dk"]), jnp.stack(samp["dv"]), jnp.stack(samp["fk"]),
            jnp.stack(samp["fv"]), jnp.stack(samp["fl"]))
```

```python
import functools
import math

import jax
import jax.numpy as jnp
from jax import lax
from jax.experimental import pallas as pl
from jax.experimental.pallas import tpu as pltpu

D_MODEL = 1024
CHUNK = 64
HEAD_DIM = 64
N_SCORE_HEADS = 16
ROPE_THETA = 10000.0
NORM_EPS = 1e-6
SUBLN_EPS = 1e-5

LANES = 128
SUBLANES = 8
N_GROUPS = D_MODEL // LANES
VMEM_LIMIT_BYTES = 56 * 1024 * 1024
ROW_BLOCK = 512
KEY_BLOCKS_PER_STEP = 2

F32 = jnp.float32
BF16 = jnp.bfloat16
NEG_INF = float("-inf")
LOG2_E = math.log2(math.e)
NT_DIMS = (((1,), (1,)), ((), ()))


def _params():
    return pltpu.CompilerParams(vmem_limit_bytes=VMEM_LIMIT_BYTES)


def _const_spec(shape):
    nd = len(shape)
    return pl.BlockSpec(shape, lambda *_: (0,) * nd, pipeline_mode=pl.Buffered(1))


def _rmsnorm_rows(x, w, eps):
    ms = jnp.mean(x * x, axis=-1, keepdims=True)
    return x * lax.rsqrt(ms + eps) * w


def _proj_body(*refs, rotary, n_prev, slot):
    if rotary:
        (x_ref, nw_ref, wq_ref, wkt_ref, wv_ref, cos_ref, sin_ref, cost_ref, sint_ref) = refs[:9]
        qh_ref, kth_ref, vh_ref, kt_ref, v_ref = refs[9 + n_prev:]
    else:
        (x_ref, nw_ref, wq_ref, wkt_ref, wvt_ref, wft_ref, bf_ref) = refs[:7]
        qh_ref, kth_ref, vth_ref, kt_ref, vt_ref, lft_ref = refs[7 + n_prev:]
    tm = x_ref.shape[1]
    for ref in refs[(9 if rotary else 7) + n_prev + 3:]:
        for other in range(ref.shape[0]):
            if other != slot:
                ref[other] = jnp.zeros(ref.shape[1:], F32)
    h = _rmsnorm_rows(x_ref[0], nw_ref[...], NORM_EPS).astype(BF16)
    q = jnp.dot(h, wq_ref[...], preferred_element_type=F32)
    kt = lax.dot_general(wkt_ref[...], h, NT_DIMS, preferred_element_type=F32)
    half = HEAD_DIM // 2
    if rotary:
        cs = cos_ref[...]
        sn = sin_ref[...]
        lane = lax.broadcasted_iota(jnp.int32, cs.shape, 1)
        first_half = (lane & half) == 0
        for g in range(N_GROUPS):
            qg = q[:, LANES * g:LANES * (g + 1)]
            qp = jnp.where(first_half, pltpu.roll(qg, LANES - half, 1), pltpu.roll(qg, half, 1))
            qh_ref[g] = (qg * cs + qp * sn).astype(BF16)
        ct = cost_ref[...]
        st = sint_ref[...]
        for hh in range(N_SCORE_HEADS):
            r0 = HEAD_DIM * hh
            a = kt[r0:r0 + half]
            b = kt[r0 + half:r0 + HEAD_DIM]
            ka = a * ct - b * st
            kb = b * ct + a * st
            kt_ref[slot, 0, r0:r0 + half, :] = ka
            kt_ref[slot, 0, r0 + half:r0 + HEAD_DIM, :] = kb
            kth_ref[0, 0, r0:r0 + half, :] = ka.astype(BF16)
            kth_ref[0, 0, r0 + half:r0 + HEAD_DIM, :] = kb.astype(BF16)
        v = jnp.dot(h, wv_ref[...], preferred_element_type=F32)
        for g in range(N_GROUPS):
            vg = v[:, LANES * g:LANES * (g + 1)]
            vh_ref[g] = vg.astype(BF16)
            v_ref[slot, pl.ds(g, tm, stride=N_GROUPS), :] = vg
    else:
        for g in range(N_GROUPS):
            qh_ref[g] = q[:, LANES * g:LANES * (g + 1)].astype(BF16)
        kt_ref[slot, 0] = kt
        kth_ref[0, 0] = kt.astype(BF16)
        vt = lax.dot_general(wvt_ref[...], h, NT_DIMS, preferred_element_type=F32)
        vt_ref[slot, 0] = vt
        vth_ref[0, 0] = vt.astype(BF16)
        z = lax.dot_general(wft_ref[...], h, NT_DIMS, preferred_element_type=F32) + bf_ref[...]
        lf = jnp.minimum(z, 0.0) - jnp.log1p(jnp.exp(-jnp.abs(z)))
        lft_ref[slot, 0] = lf[:N_SCORE_HEADS]


def _project(x3, norm_w, weights, *, rotary, tables=None, b_f=None, layer=0, n_layers=1, prev=None):
    bx, tx, _ = x3.shape
    tm = min(ROW_BLOCK, tx)
    n = bx * tx
    t_blocks = tx // tm
    grid = (t_blocks, bx)
    prev = list(prev) if prev is not None else []
    n_in_fixed = 9 if rotary else 7
    xspec = pl.BlockSpec((1, tm, D_MODEL), lambda t, b: (b, t, 0))
    wspecs = [_const_spec(w.shape) for w in weights]
    hm = jax.ShapeDtypeStruct((N_GROUPS, n, LANES), BF16)
    hm_spec = pl.BlockSpec((N_GROUPS, tm, LANES), lambda t, b: (0, b * t_blocks + t, 0))
    fm16 = jax.ShapeDtypeStruct((bx, t_blocks, D_MODEL, tm), BF16)
    fm16_spec = pl.BlockSpec((1, 1, D_MODEL, tm), lambda t, b: (b, t, 0, 0))
    fm32 = jax.ShapeDtypeStruct((n_layers, bx, D_MODEL, tx), F32)
    n_slots, lead, slot = (1, layer, 0) if prev else (n_layers, 0, layer)
    fm32_spec = pl.BlockSpec((n_slots, 1, D_MODEL, tm), lambda t, b: (lead, b, 0, t))
    if rotary:
        cos, sin, cos_t, sin_t = tables
        extra_in = [cos, sin, cos_t, sin_t]
        extra_specs = [pl.BlockSpec((tm, LANES), lambda t, b: (t, 0)),
                       pl.BlockSpec((tm, LANES), lambda t, b: (t, 0)),
                       pl.BlockSpec((HEAD_DIM // 2, tm), lambda t, b: (0, t)),
                       pl.BlockSpec((HEAD_DIM // 2, tm), lambda t, b: (0, t))]
        out_shape = [hm, fm16, hm, fm32,
                     jax.ShapeDtypeStruct((n_layers, n * N_GROUPS, LANES), F32)]
        out_specs = [hm_spec, fm16_spec, hm_spec, fm32_spec,
                     pl.BlockSpec((n_slots, tm * N_GROUPS, LANES), lambda t, b: (lead, b * t_blocks + t, 0))]
        n_f32 = 2
    else:
        extra_in = [b_f]
        extra_specs = [_const_spec(b_f.shape)]
        out_shape = [hm, fm16, fm16, fm32, fm32,
                     jax.ShapeDtypeStruct((n_layers, bx, N_SCORE_HEADS, tx), F32)]
        out_specs = [hm_spec, fm16_spec, fm16_spec, fm32_spec, fm32_spec,
                     pl.BlockSpec((n_slots, 1, N_SCORE_HEADS, tm), lambda t, b: (lead, b, 0, t))]
        n_f32 = 3
    assert len(prev) in (0, n_f32)
    aliases = {n_in_fixed + k: 3 + k for k in range(len(prev))}
    return pl.pallas_call(
        functools.partial(_proj_body, rotary=rotary, n_prev=len(prev), slot=slot),
        grid=grid,
        in_specs=[xspec, _const_spec((1, D_MODEL))] + wspecs + extra_specs
                 + [pl.BlockSpec(memory_space=pl.ANY)] * len(prev),
        out_specs=out_specs,
        out_shape=out_shape,
        input_output_aliases=aliases,
        compiler_params=_params(),
    )(x3, norm_w, *weights, *extra_in, *prev)


def _cumsum_body(x_ref, o_ref, *, chunk):
    t = x_ref.shape[2]
    r = lax.broadcasted_iota(jnp.int32, (chunk, chunk), 0)
    c = lax.broadcasted_iota(jnp.int32, (chunk, chunk), 1)
    upper = (r <= c).astype(BF16)
    carry = jnp.zeros((x_ref.shape[1], 1), F32)
    for i in range(t // chunk):
        x = x_ref[0, :, chunk * i:chunk * (i + 1)]
        hi = x.astype(BF16)
        r1 = x - hi.astype(F32)
        mid = r1.astype(BF16)
        lo = (r1 - mid.astype(F32)).astype(BF16)
        cs = (jnp.dot(hi, upper, preferred_element_type=F32)
              + jnp.dot(mid, upper, preferred_element_type=F32)
              + jnp.dot(lo, upper, preferred_element_type=F32)) + carry
        o_ref[0, :, chunk * i:chunk * (i + 1)] = cs * LOG2_E
        carry = cs[:, chunk - 1:chunk]


def _cumsum_time(lf_t, chunk):
    b, h, t = lf_t.shape
    return pl.pallas_call(
        functools.partial(_cumsum_body, chunk=chunk),
        grid=(b,),
        in_specs=[pl.BlockSpec((1, h, t), lambda i: (i, 0, 0))],
        out_specs=pl.BlockSpec((1, h, t), lambda i: (i, 0, 0)),
        out_shape=jax.ShapeDtypeStruct((b, h, t), F32),
        compiler_params=_params(),
    )(lf_t)


def _diff_lambda(lam_ref, lam_init):
    lp = lam_ref[...]
    a = jnp.sum(lp[0:1] * lp[1:2], axis=-1, keepdims=True)
    b = jnp.sum(lp[2:3] * lp[3:4], axis=-1, keepdims=True)
    return jnp.exp(a) - jnp.exp(b) + lam_init


def _diff_combine(o1, o2, lam, subln_w, lam_init):
    o = o1 - lam * o2
    return _rmsnorm_rows(o, subln_w, SUBLN_EPS) * (1.0 - lam_init)


def _stack_heads(qb):
    low = lax.broadcasted_iota(jnp.int32, qb.shape, 1) < HEAD_DIM
    zero = jnp.zeros_like(qb)
    return jnp.concatenate([jnp.where(low, qb, zero), jnp.where(low, zero, qb)], axis=0)


def _head_rows(x, t, r0):
    if r0 == 0:
        return x
    return jnp.concatenate([x[r0:t], x[t + r0:]], axis=0)


def _update_head_rows(x, t, r0, y, op):
    if r0 == 0:
        return op(x, y)
    h = t - r0
    return jnp.concatenate([x[:r0], op(x[r0:t], y[:h]), x[t:t + r0], op(x[t + r0:], y[h:])], axis=0)


def _fold_lanes(x, op):
    acc = x[:, :LANES]
    for c in range(1, x.shape[1] // LANES):
        acc = op(acc, x[:, LANES * c:LANES * (c + 1)])
    return acc


def _merge_heads(o2, t, diff, lam, subln_w, lam_init):
    o_a, o_b = o2[:t], o2[t:]
    if diff:
        out = _diff_combine(o_a, o_b, lam, subln_w, lam_init)
    else:
        low = lax.broadcasted_iota(jnp.int32, o_a.shape, 1) < HEAD_DIM
        out = jnp.where(low, o_a, o_b)
    return out.astype(BF16)


def _prompt_attn_body(*refs, diff, lam_init, blk):
    if diff:
        q_ref, kt_ref, v_ref, lam_ref, sub_ref, o_ref, s_ref, va_ref = refs
        va_ref[:, :LANES] = v_ref[0]
        va_ref[:, LANES:] = jnp.ones((va_ref.shape[0], LANES), BF16)
    else:
        q_ref, kt_ref, vt_ref, ck_ref, o_ref, s_ref, va_ref = refs
        va_ref[:, :LANES, :] = vt_ref[0]
        va_ref[:, LANES:, :] = jnp.ones((va_ref.shape[0], LANES, va_ref.shape[2]), BF16)
    t = q_ref.shape[1]
    g = pl.program_id(0)
    bq = KEY_BLOCKS_PER_STEP * blk
    if diff:
        lam = _diff_lambda(lam_ref, lam_init)
        subln_w = sub_ref[...]
    else:
        lam = subln_w = None
    neg_inf = jnp.full((2 * bq, LANES), NEG_INF, F32)

    def q_step(i, _):
        q0 = pl.multiple_of(i * bq, bq)
        q2 = _stack_heads(q_ref[0, pl.ds(q0, bq), :])
        if not diff:
            rows = [ck_ref[0, 2 * g + hd, pl.ds(KEY_BLOCKS_PER_STEP * i + u, 1), :]
                    for hd in range(2) for u in range(KEY_BLOCKS_PER_STEP)]
            pad = jnp.zeros((SUBLANES - len(rows), blk), F32)
            cols = jnp.concatenate(rows + [pad], axis=0).T
            cq2 = jnp.concatenate([cols[:, k:k + 1] for k in range(len(rows))], axis=0)

        def score(c, r0=0):
            q_sub = _head_rows(q2, bq, r0)
            s = jnp.dot(q_sub, kt_ref[0, c], preferred_element_type=F32)
            if not diff:
                h = bq - r0
                ck_a = ck_ref[0, 2 * g, pl.ds(c, 1), :]
                ck_b = ck_ref[0, 2 * g + 1, pl.ds(c, 1), :]
                s = jnp.concatenate([s[:h] + (cq2[r0:bq] - ck_a), s[h:] + (cq2[bq + r0:] - ck_b)], axis=0)
            return s

        def weigh(c, mrow_sub, r0=0):
            h = bq - r0
            if r0 == 0:
                sc = s_ref[c]
            else:
                sc = jnp.concatenate([s_ref[c, r0:bq, :], s_ref[c, bq + r0:, :]], axis=0)
            p = jnp.exp2(sc - mrow_sub).astype(BF16)
            if diff:
                k0 = pl.multiple_of(c * blk, blk)
                return jnp.dot(p, va_ref[pl.ds(k0, blk), :], preferred_element_type=F32)
            return lax.dot_general(p, va_ref[c], NT_DIMS, preferred_element_type=F32)

        def pass1(j, m):
            for u in range(KEY_BLOCKS_PER_STEP):
                c = KEY_BLOCKS_PER_STEP * j + u
                s = score(c)
                s_ref[c] = s
                m = jnp.maximum(m, _fold_lanes(s, jnp.maximum))
            return m

        m = lax.fori_loop(0, i, pass1, neg_inf)
        for u in range(KEY_BLOCKS_PER_STEP):
            c = KEY_BLOCKS_PER_STEP * i + u
            r0 = u * blk
            h = bq - r0
            qpos = r0 + lax.broadcasted_iota(jnp.int32, (2 * h, blk), 0) % h
            kpos = r0 + lax.broadcasted_iota(jnp.int32, (2 * h, blk), 1)
            allowed = (kpos // CHUNK) <= (qpos // CHUNK) if diff else kpos <= qpos
            s = jnp.where(allowed, score(c, r0), NEG_INF)
            s_ref[c, r0:bq, :] = s[:h]
            s_ref[c, bq + r0:, :] = s[h:]
            m = _update_head_rows(m, bq, r0, _fold_lanes(s, jnp.maximum), jnp.maximum)
        mrow = jnp.max(m, axis=-1, keepdims=True)

        def pass2(j, acc):
            for u in range(KEY_BLOCKS_PER_STEP):
                acc = acc + weigh(KEY_BLOCKS_PER_STEP * j + u, mrow)
            return acc

        acc = lax.fori_loop(0, i, pass2, jnp.zeros((2 * bq, 2 * LANES), F32))
        for u in range(KEY_BLOCKS_PER_STEP):
            r0 = u * blk
            pv = weigh(KEY_BLOCKS_PER_STEP * i + u, _head_rows(mrow, bq, r0), r0)
            acc = _update_head_rows(acc, bq, r0, pv, jnp.add)
        o2 = acc[:, :LANES] / acc[:, LANES:]
        o_ref[0, pl.ds(q0, bq), :] = _merge_heads(o2, bq, diff, lam, subln_w, lam_init)
        return 0

    lax.fori_loop(0, t // bq, q_step, 0)


def _prompt_attention(qh, kth, v_op, batch, *, diff, lam_init=0.0, lam_params=None, subln_w=None,
                      ck=None):
    n = qh.shape[1]
    t = n // batch
    blk = kth.shape[3]
    n_blk = t // blk
    tok = pl.BlockSpec((1, t, LANES), lambda g, b: (g, b, 0))
    feat = pl.BlockSpec((1, n_blk, LANES, blk), lambda g, b: (b, 0, g, 0))
    if diff:
        extra_in = [lam_params, subln_w]
        extra_specs = [_const_spec(lam_params.shape), _const_spec(subln_w.shape)]
        v_spec = tok
    else:
        extra_in = [ck]
        extra_specs = [pl.BlockSpec((1, N_SCORE_HEADS, n_blk, blk), lambda g, b: (b, 0, 0, 0))]
        v_spec = feat
    return pl.pallas_call(
        functools.partial(_prompt_attn_body, diff=diff, lam_init=lam_init, blk=blk),
        grid=(N_GROUPS, batch),
        in_specs=[tok, feat, v_spec] + extra_specs,
        out_specs=tok,
        out_shape=jax.ShapeDtypeStruct((N_GROUPS, n, LANES), BF16),
        scratch_shapes=[pltpu.VMEM((n_blk, 2 * KEY_BLOCKS_PER_STEP * blk, blk), F32),
                        pltpu.VMEM((t, 2 * LANES) if diff else (n_blk, 2 * LANES, blk), BF16)],
        compiler_params=_params(),
    )(qh, kth, v_op, *extra_in)


def _sample_attn_body(*refs, diff, lam_init):
    if diff:
        q_ref, ktn_ref, vn_ref, ktc_ref, vc_ref, lam_ref, sub_ref, o_ref = refs
        lam = _diff_lambda(lam_ref, lam_init)
        subln_w = sub_ref[...]
    else:
        q_ref, ktn_ref, vtn_ref, ktc_ref, vtc_ref, cq_ref, cn_ref, cc_ref, o_ref = refs
        lam = subln_w = None
    ts = q_ref.shape[1]
    p_len = ktc_ref.shape[3]
    if not diff:
        row = lax.broadcasted_iota(jnp.int32, (2 * ts, ts), 0) % ts
        col = lax.broadcasted_iota(jnp.int32, (2 * ts, ts), 1)
        causal = col <= row

    def group(g, _):
        r0 = pl.multiple_of(g * LANES, LANES)
        q2 = _stack_heads(q_ref[g])
        ktc = ktc_ref[0, 0, pl.ds(r0, LANES), :].astype(BF16)
        sc = jnp.dot(q2, ktc, preferred_element_type=F32)
        sn = jnp.dot(q2, ktn_ref[0, pl.ds(r0, LANES), :], preferred_element_type=F32)
        if not diff:
            cq = cq_ref[0, g]
            cq2 = jnp.concatenate([cq[:, 0:1], cq[:, 1:2]], axis=0)
            cc_a = cc_ref[0, pl.ds(2 * g, 1), :]
            cc_b = cc_ref[0, pl.ds(2 * g + 1, 1), :]
            cn_a = cn_ref[0, pl.ds(2 * g, 1), :]
            cn_b = cn_ref[0, pl.ds(2 * g + 1, 1), :]
            sc = jnp.concatenate(
                [sc[:ts] + ((cq2[:ts] + cc_a[:, p_len - 1:p_len]) - cc_a),
                 sc[ts:] + ((cq2[ts:] + cc_b[:, p_len - 1:p_len]) - cc_b)], axis=0)
            sn = jnp.concatenate([sn[:ts] + (cq2[:ts] - cn_a), sn[ts:] + (cq2[ts:] - cn_b)], axis=0)
            sn = jnp.where(causal, sn, NEG_INF)
        m = jnp.maximum(jnp.max(sc, axis=-1, keepdims=True), jnp.max(sn, axis=-1, keepdims=True))
        pc = jnp.exp2(sc - m)
        pn = jnp.exp2(sn - m)
        l = jnp.sum(pc, axis=-1, keepdims=True) + jnp.sum(pn, axis=-1, keepdims=True)
        if diff:
            vc = vc_ref[0, 0, pl.ds(g, p_len, stride=N_GROUPS), :].astype(BF16)
            acc = (jnp.dot(pc.astype(BF16), vc, preferred_element_type=F32)
                   + jnp.dot(pn.astype(BF16), vn_ref[g], preferred_element_type=F32))
        else:
            vtc = vtc_ref[0, 0, pl.ds(r0, LANES), :].astype(BF16)
            acc = (lax.dot_general(pc.astype(BF16), vtc, NT_DIMS, preferred_element_type=F32)
                   + lax.dot_general(pn.astype(BF16), vtn_ref[0, pl.ds(r0, LANES), :], NT_DIMS,
                                     preferred_element_type=F32))
        o_ref[g] = _merge_heads(acc / l, ts, diff, lam, subln_w, lam_init)
        return 0

    lax.fori_loop(0, N_GROUPS, group, 0)


def _sample_attention(qh, ktn, v_new, cache_kt, cache_v, layer, *, diff, lam_init=0.0, lam_params=None,
                      subln_w=None, cq=None, cn=None, cc=None):
    _, batch, _, p_len = cache_kt.shape
    n = qh.shape[1]
    ts = n // batch
    tok = pl.BlockSpec((N_GROUPS, ts, LANES), lambda b: (0, b, 0))
    feat = pl.BlockSpec((1, D_MODEL, ts), lambda b: (b, 0, 0))
    cache_feat = pl.BlockSpec((1, 1, D_MODEL, p_len), lambda b: (layer, b, 0, 0))
    if diff:
        extra_in = [lam_params, subln_w]
        extra_specs = [_const_spec(lam_params.shape), _const_spec(subln_w.shape)]
        v_specs = [tok, cache_feat, pl.BlockSpec((1, 1, p_len * N_GROUPS, LANES), lambda b: (layer, b, 0, 0))]
    else:
        extra_in = [cq, cn, cc]
        extra_specs = [pl.BlockSpec((1, N_GROUPS, ts, 2), lambda b: (b, 0, 0, 0)),
                       pl.BlockSpec((1, N_SCORE_HEADS, ts), lambda b: (b, 0, 0)),
                       pl.BlockSpec((1, N_SCORE_HEADS, p_len), lambda b: (b, 0, 0))]
        v_specs = [feat, cache_feat, cache_feat]
    return pl.pallas_call(
        functools.partial(_sample_attn_body, diff=diff, lam_init=lam_init),
        grid=(batch,),
        in_specs=[tok, feat] + v_specs + extra_specs,
        out_specs=tok,
        out_shape=jax.ShapeDtypeStruct((N_GROUPS, n, LANES), BF16),
        compiler_params=_params(),
    )(qh, ktn, v_new, cache_kt, cache_v, *extra_in)


def _mlp_body(x_ref, o_ref, wo_ref, nw_ref, win_ref, wout_ref, nf_ref, y_ref, *, final):
    d_ff = wout_ref.shape[0]
    o = jnp.concatenate([o_ref[g] for g in range(N_GROUPS)], axis=1)
    x1 = x_ref[...] + jnp.dot(o, wo_ref[...], preferred_element_type=F32)
    h = _rmsnorm_rows(x1, nw_ref[...], NORM_EPS).astype(BF16)
    gate = jnp.dot(h, win_ref[:, :d_ff], preferred_element_type=F32)
    up = jnp.dot(h, win_ref[:, d_ff:], preferred_element_type=F32)
    act = (gate * jax.nn.sigmoid(gate) * up).astype(BF16)
    x2 = x1 + jnp.dot(act, wout_ref[...], preferred_element_type=F32)
    if final:
        x2 = _rmsnorm_rows(x2, nf_ref[...], NORM_EPS)
    y_ref[...] = x2


def _mlp(x2d, o_hm, w_o, norm_w, w_in, w_out, norm_final, *, final):
    n = x2d.shape[0]
    tm = min(ROW_BLOCK, n)
    d_ff = w_out.shape[0]
    return pl.pallas_call(
        functools.partial(_mlp_body, final=final),
        grid=(n // tm,),
        in_specs=[pl.BlockSpec((tm, D_MODEL), lambda i: (i, 0)),
                  pl.BlockSpec((N_GROUPS, tm, LANES), lambda i: (0, i, 0)),
                  _const_spec((D_MODEL, D_MODEL)),
                  _const_spec((1, D_MODEL)),
                  _const_spec((D_MODEL, 2 * d_ff)),
                  _const_spec((d_ff, D_MODEL)),
                  _const_spec((1, D_MODEL))],
        out_specs=pl.BlockSpec((tm, D_MODEL), lambda i: (i, 0)),
        out_shape=jax.ShapeDtypeStruct((n, D_MODEL), F32),
        compiler_params=_params(),
    )(x2d, o_hm, w_o, norm_w, w_in, w_out, norm_final)


def _rope_tables(pos):
    half = HEAD_DIM // 2
    inv_freq = ROPE_THETA ** (-jnp.arange(half, dtype=F32) / half)
    ang = pos.astype(F32)[:, None] * inv_freq[None, :]
    cos = jnp.cos(ang)
    sin = jnp.sin(ang)
    cos_tok = jnp.tile(cos, (1, LANES // half))
    sin_tok = jnp.tile(jnp.concatenate([-sin, sin], axis=1), (1, LANES // HEAD_DIM))
    return cos_tok, sin_tok, cos.T, sin.T


def _lambda_init(layer):
    return 0.8 - 0.6 * math.exp(-0.3 * layer)


def _pair_columns(c_t):
    b, _, t = c_t.shape
    return jnp.swapaxes(c_t.reshape(b, N_GROUPS, 2, t), 2, 3)


def _feature_major_to_tokens(a, heads):
    *lead, b, _, t = a.shape
    a = a.reshape(*lead, b, heads, D_MODEL // heads, t)
    return jnp.moveaxis(a, -1, -3)


def kernel(x_prompt, x_sample, cache_diff_k, cache_diff_v, cache_fox_k, cache_fox_v, cache_fox_logf,
           norm_mix, norm_ffn, norm_final, diff_w_qkv, diff_w_o, diff_lambda, diff_subln,
           fox_w_qkvf, fox_b_f, fox_w_o, ffn_w_in, ffn_w_out):
    bp, s_len, _ = x_prompt.shape
    bs, t_len, _ = x_sample.shape
    p_len = cache_diff_k.shape[2]
    depth = norm_mix.shape[0]
    n_diff = diff_w_qkv.shape[0]
    n_fox = fox_w_qkvf.shape[0]
    dv_heads = cache_diff_v.shape[3]

    ns = bs * t_len
    tm_s = min(ROW_BLOCK, ns)
    xp3 = x_prompt
    xs3 = x_sample.reshape(ns // tm_s, tm_s, D_MODEL)
    xp = x_prompt.reshape(bp * s_len, D_MODEL)
    xs = x_sample.reshape(ns, D_MODEL)

    tab_p = _rope_tables(jnp.arange(s_len, dtype=jnp.int32))
    tab_s = _rope_tables(p_len + jnp.arange(t_len, dtype=jnp.int32))
    reps = tm_s // t_len
    tab_s = (jnp.tile(tab_s[0], (reps, 1)), jnp.tile(tab_s[1], (reps, 1)),
             jnp.tile(tab_s[2], (1, reps)), jnp.tile(tab_s[3], (1, reps)))

    q_scale = HEAD_DIM ** -0.5 * LOG2_E
    d = D_MODEL
    wd = [((diff_w_qkv[j, :, :d] * q_scale).astype(BF16),
           diff_w_qkv[j, :, d:2 * d].T.astype(BF16),
           diff_w_qkv[j, :, 2 * d:].astype(BF16)) for j in range(n_diff)]
    pad = LANES - N_SCORE_HEADS
    wf = [((fox_w_qkvf[j, :, :d] * q_scale).astype(BF16),
           fox_w_qkvf[j, :, d:2 * d].T.astype(BF16),
           fox_w_qkvf[j, :, 2 * d:3 * d].T.astype(BF16),
           jnp.concatenate([fox_w_qkvf[j, :, 3 * d:].T, jnp.zeros((pad, d), F32)], axis=0).astype(BF16))
          for j in range(n_fox)]
    b_f = [jnp.concatenate([fox_b_f[j], jnp.zeros((pad,), F32)]).reshape(LANES, 1) for j in range(n_fox)]
    w_o_d = diff_w_o.astype(BF16)
    w_o_f = fox_w_o.astype(BF16)
    w_in = ffn_w_in.astype(BF16)
    w_out = ffn_w_out.astype(BF16)
    norm_mix = norm_mix.reshape(depth, 1, D_MODEL)
    norm_ffn = norm_ffn.reshape(depth, 1, D_MODEL)
    norm_final = norm_final.reshape(1, D_MODEL)

    def feature_major(c):
        l, b, p, hh, dd = c.shape
        return jnp.transpose(c, (0, 1, 3, 4, 2)).reshape(l, b, hh * dd, p)

    ckd = feature_major(cache_diff_k)
    cvd = cache_diff_v.reshape(n_diff, bs, p_len * dv_heads, LANES)
    ckf = feature_major(cache_fox_k)
    cvf = feature_major(cache_fox_v)

    def sample_feature_major(a):
        a = a.reshape(-1, D_MODEL, tm_s // t_len, t_len)
        return jnp.transpose(a, (0, 2, 1, 3)).reshape(bs, D_MODEL, t_len)

    prev_d = prev_f = None
    samp = {k: [] for k in ("dk", "dv", "fk", "fv", "fl")}
    for i in range(depth):
        j = i // 2
        final = i == depth - 1
        if i % 2 == 0:
            lam_init = _lambda_init(i)
            sub = diff_subln[j].reshape(1, LANES)
            qh, kth, vh, kt32, v32 = _project(xp3, norm_mix[i], wd[j], rotary=True, tables=tab_p,
                                              layer=j, n_layers=n_diff, prev=prev_d)
            prev_d = (kt32, v32)
            op = _prompt_attention(qh, kth, vh, bp, diff=True, lam_init=lam_init,
                                   lam_params=diff_lambda[j], subln_w=sub)
            qh, kth, vh, kt32, v32 = _project(xs3, norm_mix[i], wd[j], rotary=True, tables=tab_s)
            os_ = _sample_attention(qh, sample_feature_major(kth), vh, ckd, cvd, j, diff=True,
                                    lam_init=lam_init, lam_params=diff_lambda[j], subln_w=sub)
            samp["dk"].append(_feature_major_to_tokens(sample_feature_major(kt32[0]), N_SCORE_HEADS))
            samp["dv"].append(v32[0].reshape(bs, t_len, dv_heads, LANES))
            w_o = w_o_d[j]
        else:
            qh, kth, vth, kt32, vt32, lft = _project(xp3, norm_mix[i], wf[j], rotary=False, b_f=b_f[j],
                                                    layer=j, n_layers=n_fox, prev=prev_f)
            prev_f = (kt32, vt32, lft)
            c_t = _cumsum_time(lft[j], min(512, s_len))
            ck = c_t.reshape(bp, N_SCORE_HEADS, kth.shape[1], kth.shape[3])
            op = _prompt_attention(qh, kth, vth, bp, diff=False, ck=ck)
            qh, kth, vth, kt32, vt32, lft_s = _project(xs3, norm_mix[i], wf[j], rotary=False, b_f=b_f[j])
            lf_s = jnp.transpose(lft_s[0].reshape(ns // tm_s, N_SCORE_HEADS, tm_s // t_len, t_len),
                                 (0, 2, 1, 3)).reshape(bs, N_SCORE_HEADS, t_len)
            cc = _cumsum_time(jnp.swapaxes(cache_fox_logf[j], 1, 2), min(512, p_len))
            cn = _cumsum_time(lf_s, t_len)
            os_ = _sample_attention(qh, sample_feature_major(kth), sample_feature_major(vth), ckf, cvf, j,
                                    diff=False, cq=_pair_columns(cn), cn=cn, cc=cc)
            samp["fk"].append(_feature_major_to_tokens(sample_feature_major(kt32[0]), N_SCORE_HEADS))
            samp["fv"].append(_feature_major_to_tokens(sample_feature_major(vt32[0]), N_SCORE_HEADS))
            samp["fl"].append(jnp.swapaxes(lf_s, 1, 2))
            w_o = w_o_f[j]
        xp = _mlp(xp, op, w_o, norm_ffn[i], w_in[i], w_out[i], norm_final, final=final)
        xs = _mlp(xs, os_, w_o, norm_ffn[i], w_in[i], w_out[i], norm_final, final=final)
        xp3 = xp.reshape(bp, s_len, D_MODEL)
        xs3 = xs.reshape(ns // tm_s, tm_s, D_MODEL)

    dk32, dv32 = prev_d
    fk32, fv32, fl32 = prev_f
    return (xp.reshape(bp, s_len, D_MODEL), xs.reshape(bs, t_len, D_MODEL),
            _feature_major_to_tokens(dk32, N_SCORE_HEADS),
            dv32.reshape(n_diff, bp, s_len, dv_heads, LANES),
            _feature_major_to_tokens(fk32, N_SCORE_HEADS),
            _feature_major_to_tokens(fv32, N_SCORE_HEADS),
            jnp.swapaxes(fl32, 2, 3),
            jnp.stack(samp["dk"]), jnp.stack(samp["dv"]), jnp.stack(samp["fk"]),
            jnp.stack(samp["fv"]), jnp.stack(samp["fl"]))
```
